```python
import math
import jax
import jax.numpy as jnp
from jax import lax
import numpy as np

D_MODEL = 1024
BATCH = 8
SEQ = 2048
DEPTH = 4
DEC_BATCH = 128
DEC_SEQ = 8
PAST_LEN = 2048
PAGE_SIZE = 128

HEAD_DIM = 64
D_MIX = D_MODEL
D_A = D_MIX // 2
D_B = D_MIX // 4
D_C = D_MIX - D_A - D_B
N_HEADS_A = D_A // HEAD_DIM
N_HEADS_B = D_B // HEAD_DIM
N_HEADS_C = D_C // HEAD_DIM
MOBA_BLOCK = 256
MOBA_TOPK = 3
HGRN_CHUNK = 16
MLSTM_CHUNK = 64
CONV_W = 4
D_IN = 4 * D_A + 4 * D_B + 2 * D_C
GATHER_ROWS = 128
EPS = 1e-6

kernel_name = 'moba_hgrn2_mlstm_parallel_hybrid'


def rms_norm(x, g):
    xf = x.astype(jnp.float32)
    y = xf * lax.rsqrt(jnp.mean(xf * xf, axis=-1, keepdims=True) + EPS)
    return (y * g.astype(jnp.float32)).astype(x.dtype)


def head_layer_norm(x, g):
    xf = x.astype(jnp.float32)
    xc = xf - jnp.mean(xf, axis=-1, keepdims=True)
    y = xc * lax.rsqrt(jnp.mean(xc * xc, axis=-1, keepdims=True) + EPS)
    return (y * g.astype(jnp.float32)).astype(x.dtype)


def alibi_slopes(n_heads):
    return jnp.asarray(2.0 ** (-8.0 * (np.arange(n_heads) + 1) / n_heads), dtype=jnp.float32)


def to_chunks(a, L):
    B, T = a.shape[:2]
    a = a.reshape((B, T // L, L) + a.shape[2:])
    return a.transpose((1, 0, 3, 2) + tuple(range(4, a.ndim)))


def from_chunks(o):
    nc, B, H, L, d = o.shape
    return o.transpose(1, 0, 3, 2, 4).reshape(B, nc * L, H, d)


def moba_attention(q, k, v, q_pos):
    B, Tq, H, dh = q.shape
    Tk = k.shape[1]
    nb = -(-Tk // MOBA_BLOCK)
    pad = ((0, 0), (0, nb * MOBA_BLOCK - Tk), (0, 0), (0, 0))
    kb = jnp.pad(k, pad).reshape(B, nb, MOBA_BLOCK, H, dh)
    vb = jnp.pad(v, pad).reshape(B, nb, MOBA_BLOCK, H, dh)
    k_mean = jnp.mean(kb.astype(jnp.float32), axis=2)
    gate = jnp.einsum('bthd,bnhd->bhtn', q.astype(jnp.float32), k_mean)
    q_blk = q_pos // MOBA_BLOCK
    fully_past = jnp.arange(nb, dtype=jnp.int32)[None, :] < q_blk[:, None]
    gate = jnp.where(fully_past, gate, -jnp.inf)
    top_val, top_idx = lax.top_k(gate, min(MOBA_TOPK, nb))
    own_idx = jnp.broadcast_to(q_blk[None, None, :, None], (B, H, Tq, 1)).astype(top_idx.dtype)
    blk_idx = jnp.concatenate([top_idx, own_idx], axis=-1)
    blk_ok = jnp.concatenate([jnp.isfinite(top_val), jnp.ones((B, H, Tq, 1), dtype=bool)], axis=-1)
    qc = math.gcd(Tq, max(1, GATHER_ROWS // B))
    nc = Tq // qc

    def split_q(a):
        return jnp.moveaxis(a.reshape((B, H, nc, qc) + a.shape[3:]), 2, 0)

    q_chunks = jnp.moveaxis(q.reshape(B, nc, qc, H, dh), 1, 0)
    pos_chunks = q_pos.reshape(nc, qc)
    b_ix = jnp.arange(B)[:, None, None, None]
    h_ix = jnp.arange(H)[None, :, None, None]
    offs = jnp.arange(MOBA_BLOCK, dtype=jnp.int32)
    slopes = alibi_slopes(H)[None, :, None, None, None]
    scale = dh ** -0.5

    def attend(args):
        q_c, idx_c, ok_c, pos_c = args
        k_g = kb[b_ix, idx_c, :, h_ix]
        v_g = vb[b_ix, idx_c, :, h_ix]
        dist = pos_c[None, None, :, None, None] - (idx_c[..., None] * MOBA_BLOCK + offs)
        s = jnp.einsum('bqhd,bhqjpd->bhqjp', q_c, k_g).astype(jnp.float32) * scale
        s = s - slopes * dist.astype(jnp.float32)
        s = jnp.where(ok_c[..., None] & (dist >= 0), s, -jnp.inf)
        p = jax.nn.softmax(s.reshape(s.shape[:3] + (-1,)), axis=-1).reshape(s.shape)
        return jnp.einsum('bhqjp,bhqjpd->bqhd', p.astype(v_g.dtype), v_g)

    out = lax.map(attend, (q_chunks, split_q(blk_idx), split_q(blk_ok), pos_chunks))
    return jnp.moveaxis(out, 0, 1).reshape(B, Tq, H, dh)


def hgrn2_recurrence(q, logf, k, v, s0):
    T = q.shape[1]
    L = math.gcd(T, HGRN_CHUNK)
    causal = jnp.tril(jnp.ones((L, L), dtype=bool))[None, None, :, :, None]

    def step(s, inp):
        q_c, lf_c, k_c, v_c = inp
        b = jnp.cumsum(lf_c, axis=2)
        o_inter = jnp.einsum('bhtk,bhkv->bhtv', q_c * jnp.exp(b), s)
        rel = jnp.exp(jnp.where(causal, b[:, :, :, None, :] - b[:, :, None, :, :], -jnp.inf))
        a = jnp.einsum('bhtk,bhsk,bhtsk->bhts', q_c, k_c, rel)
        o = o_inter + jnp.einsum('bhts,bhsv->bhtv', a, v_c)
        b_end = b[:, :, -1:, :]
        s_new = jnp.exp(b_end[:, :, 0, :, None]) * s + jnp.einsum('bhsk,bhsv->bhkv', k_c * jnp.exp(b_end - b), v_c)
        return s_new, o

    xs = tuple(to_chunks(a.astype(jnp.float32), L) for a in (q, logf, k, v))
    s_fin, o = lax.scan(step, s0.astype(jnp.float32), xs)
    return from_chunks(o), s_fin


def mlstm_recurrence(q, k, v, i_raw, logf, c0, n0, m0):
    T = q.shape[1]
    L = math.gcd(T, MLSTM_CHUNK)
    causal = jnp.tril(jnp.ones((L, L), dtype=bool))

    def step(carry, inp):
        c, n, m = carry
        q_c, k_c, v_c, i_c, lf_c = inp
        b = jnp.cumsum(lf_c, axis=-1)
        m_t = b + jnp.maximum(m[..., None], lax.cummax(i_c - b, axis=2))
        g = jnp.exp(b + m[..., None] - m_t)
        dmat = jnp.exp(jnp.where(causal, b[..., :, None] - b[..., None, :] + i_c[..., None, :] - m_t[..., :, None], -jnp.inf))
        qk = jnp.einsum('bhtd,bhsd->bhts', q_c, k_c) * dmat
        num = g[..., None] * jnp.einsum('bhtk,bhkv->bhtv', q_c, c) + jnp.einsum('bhts,bhsv->bhtv', qk, v_c)
        den = g * jnp.einsum('bhtk,bhk->bht', q_c, n) + jnp.sum(qk, axis=-1)
        h = num / jnp.maximum(jnp.abs(den), jnp.exp(-m_t))[..., None]
        m_end = m_t[..., -1]
        w = jnp.exp(b[..., -1:] - b + i_c - m_end[..., None])
        g_end = jnp.exp(b[..., -1] + m - m_end)
        c_new = g_end[..., None, None] * c + jnp.einsum('bhs,bhsk,bhsv->bhkv', w, k_c, v_c)
        n_new = g_end[..., None] * n + jnp.einsum('bhs,bhsk->bhk', w, k_c)
        return (c_new, n_new, m_end), h

    xs = tuple(to_chunks(a.astype(jnp.float32), L) for a in (q, k, v, i_raw, logf))
    init = (c0.astype(jnp.float32), n0.astype(jnp.float32), m0.astype(jnp.float32))
    (c_fin, n_fin, m_fin), h = lax.scan(step, init, xs)
    return from_chunks(h), c_fin, n_fin, m_fin


def mixer_layer(x, k_past, v_past, s0, c0, n0, m0, conv0, norm_g, w_in, q_norm_g, k_norm_g, lb,
                hgrn_norm_g, conv_w, conv_b, wq, wk, wv, w_ig, b_ig, w_fg, b_fg, skip, mlstm_norm_g, w_out):
    B, T, _ = x.shape
    dh = HEAD_DIM
    h = rms_norm(x, norm_g)
    proj = h @ w_in
    sizes = (D_A,) * 4 + (D_B,) * 4 + (D_C,) * 2
    cuts = [int(c) for c in np.cumsum(sizes)[:-1]]
    qa, ka, va, za, qb, fb, ib, zb, uc, zc = jnp.split(proj, cuts, axis=-1)

    qa = rms_norm(qa.reshape(B, T, N_HEADS_A, dh), q_norm_g)
    ka = rms_norm(ka.reshape(B, T, N_HEADS_A, dh), k_norm_g)
    va = va.reshape(B, T, N_HEADS_A, dh)
    p0 = k_past.shape[1]
    k_all = jnp.concatenate([k_past.astype(ka.dtype), ka], axis=1)
    v_all = jnp.concatenate([v_past.astype(va.dtype), va], axis=1)
    q_pos = p0 + jnp.arange(T, dtype=jnp.int32)
    oa = moba_attention(qa, k_all, v_all, q_pos).reshape(B, T, D_A) * jax.nn.silu(za)

    fgate = lb + (1.0 - lb) * jax.nn.sigmoid(fb.astype(jnp.float32))
    ob, s_new = hgrn2_recurrence(qb.reshape(B, T, N_HEADS_B, dh),
                                 jnp.log(fgate).reshape(B, T, N_HEADS_B, dh),
                                 (1.0 - fgate).reshape(B, T, N_HEADS_B, dh),
                                 ib.reshape(B, T, N_HEADS_B, dh), s0)
    ob = rms_norm(ob, hgrn_norm_g.reshape(N_HEADS_B, dh)).reshape(B, T, D_B).astype(x.dtype) * jax.nn.silu(zb)

    u_ext = jnp.concatenate([conv0.astype(uc.dtype), uc], axis=1)
    conv = conv_b
    for j in range(CONV_W):
        conv = conv + conv_w[j] * u_ext[:, j:j + T]
    uconv = jax.nn.silu(conv)
    uh = uconv.reshape(B, T, N_HEADS_C, dh)
    q_m = jnp.einsum('bthd,hde->bthe', uh, wq)
    k_m = jnp.einsum('bthd,hde->bthe', uh, wk)
    v_m = jnp.einsum('bthd,hde->bthe', uc.reshape(B, T, N_HEADS_C, dh), wv)
    gate_in = jnp.concatenate([a.reshape(B, T, D_C) for a in (q_m, k_m, v_m)], axis=-1)
    i_raw = (gate_in @ w_ig + b_ig).astype(jnp.float32)
    logf = jax.nn.log_sigmoid((gate_in @ w_fg + b_fg).astype(jnp.float32))
    hc, c_new, n_new, m_new = mlstm_recurrence(q_m, k_m * dh ** -0.5, v_m, i_raw, logf, c0, n0, m0)
    hc = head_layer_norm(hc, mlstm_norm_g.reshape(N_HEADS_C, dh)).reshape(B, T, D_C).astype(x.dtype)
    oc = (hc + skip * uconv) * jax.nn.silu(zc)

    y = x + jnp.concatenate([oa, ob, oc], axis=-1) @ w_out
    dt = x.dtype
    return (y, ka, va, s_new.astype(dt), c_new.astype(dt), n_new.astype(dt), m_new.astype(dt),
            u_ext[:, T:])


def setup_inputs(seed: int = 0) -> dict:
    key = jax.random.key(seed)
    ks = iter(jax.random.split(key, 32))
    f32 = jnp.float32
    dh = HEAD_DIM
    n_pages = PAST_LEN // PAGE_SIZE
    n_used = DEC_BATCH * n_pages
    n_phys = n_used + (n_used + 3) // 4

    def nrm(shape, s=1.0):
        return jax.random.normal(next(ks), shape, f32) * s

    perm = jax.random.permutation(next(ks), n_phys)[:n_used]
    page_table = perm.reshape(DEC_BATCH, n_pages).astype(jnp.int32)
    return {
        'x_prompt': nrm((BATCH, SEQ, D_MODEL)),
        'x_sample': nrm((DEC_BATCH, DEC_SEQ, D_MODEL)),
        'cache_k': nrm((n_phys, DEPTH, PAGE_SIZE, N_HEADS_A, dh)),
        'cache_v': nrm((n_phys, DEPTH, PAGE_SIZE, N_HEADS_A, dh)),
        'page_table': page_table,
        'state_hgrn': nrm((DEPTH, DEC_BATCH, N_HEADS_B, dh, dh), 0.5),
        'state_mlstm_c': nrm((DEPTH, DEC_BATCH, N_HEADS_C, dh, dh), 0.5),
        'state_mlstm_n': nrm((DEPTH, DEC_BATCH, N_HEADS_C, dh), 0.5),
        'state_mlstm_m': nrm((DEPTH, DEC_BATCH, N_HEADS_C)),
        'state_mlstm_conv': nrm((DEPTH, DEC_BATCH, CONV_W - 1, D_C)),
        'norm_g': 1.0 + nrm((DEPTH, D_MODEL), 0.02),
        'w_in': nrm((DEPTH, D_MODEL, D_IN), D_MODEL ** -0.5),
        'q_norm_g': 1.0 + nrm((DEPTH, dh), 0.02),
        'k_norm_g': 1.0 + nrm((DEPTH, dh), 0.02),
        'hgrn_lb': nrm((DEPTH, D_B), 0.1),
        'hgrn_norm_g': 1.0 + nrm((DEPTH, D_B), 0.02),
        'mlstm_conv_w': nrm((DEPTH, CONV_W, D_C), CONV_W ** -0.5),
        'mlstm_conv_b': nrm((DEPTH, D_C), 0.01),
        'mlstm_wq': nrm((DEPTH, N_HEADS_C, dh, dh), dh ** -0.5),
        'mlstm_wk': nrm((DEPTH, N_HEADS_C, dh, dh), dh ** -0.5),
        'mlstm_wv': nrm((DEPTH, N_HEADS_C, dh, dh), dh ** -0.5),
        'mlstm_w_ig': nrm((DEPTH, 3 * D_C, N_HEADS_C), 0.1 * (3 * D_C) ** -0.5),
        'mlstm_b_ig': nrm((DEPTH, N_HEADS_C), 0.1),
        'mlstm_w_fg': nrm((DEPTH, 3 * D_C, N_HEADS_C), 0.1 * (3 * D_C) ** -0.5),
        'mlstm_b_fg': jnp.linspace(3.0, 6.0, N_HEADS_C, dtype=f32)[None, :] + nrm((DEPTH, N_HEADS_C), 0.01),
        'mlstm_skip': 1.0 + nrm((DEPTH, D_C), 0.02),
        'mlstm_norm_g': 1.0 + nrm((DEPTH, D_C), 0.02),
        'w_out': nrm((DEPTH, D_MIX, D_MODEL), 0.5 * D_MIX ** -0.5),
    }


def reference(x_prompt, x_sample, cache_k, cache_v, page_table, state_hgrn, state_mlstm_c, state_mlstm_n,
              state_mlstm_m, state_mlstm_conv, norm_g, w_in, q_norm_g, k_norm_g, hgrn_lb, hgrn_norm_g,
              mlstm_conv_w, mlstm_conv_b, mlstm_wq, mlstm_wk, mlstm_wv, mlstm_w_ig, mlstm_b_ig, mlstm_w_fg,
              mlstm_b_fg, mlstm_skip, mlstm_norm_g, w_out):
    lb_w = jax.nn.softmax(hgrn_lb.astype(jnp.float32), axis=0)
    lb_all = jnp.cumsum(lb_w, axis=0) - lb_w[0]
    bp = x_prompt.shape[0]
    bd = x_sample.shape[0]
    dt = x_prompt.dtype
    zeros_kv = jnp.zeros((bp, 0, N_HEADS_A, HEAD_DIM), dt)
    zeros_s = jnp.zeros((bp, N_HEADS_B, HEAD_DIM, HEAD_DIM), dt)
    zeros_c = jnp.zeros((bp, N_HEADS_C, HEAD_DIM, HEAD_DIM), dt)
    zeros_n = jnp.zeros((bp, N_HEADS_C, HEAD_DIM), dt)
    zeros_m = jnp.zeros((bp, N_HEADS_C), dt)
    zeros_conv = jnp.zeros((bp, CONV_W - 1, D_C), dt)
    yp, ys = x_prompt, x_sample
    new_p = [[] for _ in range(7)]
    new_s = [[] for _ in range(7)]
    for l in range(DEPTH):
        w_l = (norm_g[l], w_in[l], q_norm_g[l], k_norm_g[l], lb_all[l], hgrn_norm_g[l], mlstm_conv_w[l],
               mlstm_conv_b[l], mlstm_wq[l], mlstm_wk[l], mlstm_wv[l], mlstm_w_ig[l], mlstm_b_ig[l],
               mlstm_w_fg[l], mlstm_b_fg[l], mlstm_skip[l], mlstm_norm_g[l], w_out[l])
        yp, *out_p = mixer_layer(yp, zeros_kv, zeros_kv, zeros_s, zeros_c, zeros_n, zeros_m, zeros_conv, *w_l)
        k_past = cache_k[page_table, l].reshape(bd, -1, N_HEADS_A, HEAD_DIM)
        v_past = cache_v[page_table, l].reshape(bd, -1, N_HEADS_A, HEAD_DIM)
        ys, *out_s = mixer_layer(ys, k_past, v_past, state_hgrn[l], state_mlstm_c[l], state_mlstm_n[l],
                                 state_mlstm_m[l], state_mlstm_conv[l], *w_l)
        for acc, a in zip(new_p, out_p):
            acc.append(a)
        for acc, a in zip(new_s, out_s):
            acc.append(a)
    k_prompt = jnp.stack(new_p[0], axis=1)
    v_prompt = jnp.stack(new_p[1], axis=1)
    k_sample = jnp.stack(new_s[0], axis=1)
    v_sample = jnp.stack(new_s[1], axis=1)
    hgrn_prompt = jnp.stack(new_p[2], axis=0)
    hgrn_sample = jnp.stack(new_s[2], axis=0)
    c_prompt = jnp.stack(new_p[3], axis=0)
    c_sample = jnp.stack(new_s[3], axis=0)
    n_prompt = jnp.stack(new_p[4], axis=0)
    n_sample = jnp.stack(new_s[4], axis=0)
    m_prompt = jnp.stack(new_p[5], axis=0)
    m_sample = jnp.stack(new_s[5], axis=0)
    conv_prompt = jnp.stack(new_p[6], axis=0)
    conv_sample = jnp.stack(new_s[6], axis=0)
    return (yp, ys, k_prompt, v_prompt, k_sample, v_sample, hgrn_prompt, hgrn_sample, c_prompt, c_sample,
            n_prompt, n_sample, m_prompt, m_sample, conv_prompt, conv_sample)
```

```python
import functools
import math

import numpy as np
import jax
import jax.numpy as jnp
from jax import lax
from jax.experimental import pallas as pl
from jax.experimental.pallas import tpu as pltpu

F32 = jnp.float32
BF16 = jnp.bfloat16

HEAD_DIM = 64
N_HEADS_A = 8
N_HEADS_B = 4
N_HEADS_C = 4
D_A = N_HEADS_A * HEAD_DIM
D_B = N_HEADS_B * HEAD_DIM
D_C = N_HEADS_C * HEAD_DIM
MOBA_BLOCK = 256
MOBA_TOPK = 3
CONV_W = 4
EPS = 1e-6
NEG = -1e30
GATE_LANES = 128
VMEM_LIMIT = 56 * 1024 * 1024


def _bf(x):
    return x.astype(BF16)


def _dot(a, b):
    return jnp.dot(_bf(a), _bf(b), preferred_element_type=F32)


def _dot_nt(a, b):
    return lax.dot_general(_bf(a), _bf(b), (((1,), (1,)), ((), ())), preferred_element_type=F32)


def _dot_tn(a, b):
    return lax.dot_general(_bf(a), _bf(b), (((0,), (0,)), ((), ())), preferred_element_type=F32)


def _split2(x):
    hi = _bf(x)
    lo = _bf(x - hi.astype(F32))
    return hi, lo


def _split3(x):
    hi = _bf(x)
    r = x - hi.astype(F32)
    mid = _bf(r)
    lo = _bf(r - mid.astype(F32))
    return hi, mid, lo


def _dot_sel(w01, x):
    hi, mid, lo = _split3(x)
    return (jnp.dot(w01, hi, preferred_element_type=F32) + jnp.dot(w01, mid, preferred_element_type=F32)
            + jnp.dot(w01, lo, preferred_element_type=F32))


def _dot_x_sel(x, w01):
    hi, mid, lo = _split3(x)
    return (jnp.dot(hi, w01, preferred_element_type=F32) + jnp.dot(mid, w01, preferred_element_type=F32)
            + jnp.dot(lo, w01, preferred_element_type=F32))


def _dot3(a, b):
    ah, al = _split2(a)
    bh, bl = _split2(b)
    return (jnp.dot(ah, bh, preferred_element_type=F32) + jnp.dot(al, bh, preferred_element_type=F32)
            + jnp.dot(ah, bl, preferred_element_type=F32))


def _dot3_nt(a, b):
    ah, al = _split2(a)
    bh, bl = _split2(b)
    dn = (((1,), (1,)), ((), ()))
    return (lax.dot_general(ah, bh, dn, preferred_element_type=F32)
            + lax.dot_general(al, bh, dn, preferred_element_type=F32)
            + lax.dot_general(ah, bl, dn, preferred_element_type=F32))


def _seg_sum(x, e_ref):
    hi, lo = _split2(x)
    e = e_ref[...]
    return jnp.dot(hi, e, preferred_element_type=F32) + jnp.dot(lo, e, preferred_element_type=F32)


def _spread_heads(x, rows, n_heads):
    lane = lax.broadcasted_iota(jnp.int32, (rows, n_heads * HEAD_DIM), 1)
    out = jnp.zeros((rows, n_heads * HEAD_DIM), F32)
    for h in range(n_heads):
        in_head = (lane >= h * HEAD_DIM) & (lane < (h + 1) * HEAD_DIM)
        out = jnp.where(in_head, x[:, h:h + 1], out)
    return out


def _silu(x):
    return x * jax.nn.sigmoid(x)


def _log_sigmoid(x):
    return jnp.minimum(x, 0.0) - jnp.log(1.0 + jnp.exp(-jnp.abs(x)))


def _const_spec(shape):
    nd = len(shape)
    return pl.BlockSpec(shape, lambda *_: (0,) * nd)


def _head_block_ones():
    r = np.arange(256)
    return jnp.asarray((r[:, None] // HEAD_DIM) == (r[None, :] // HEAD_DIM), dtype=BF16)


def _block_diag(w):
    h = w.shape[0]
    eye = jnp.eye(h, dtype=w.dtype)
    return jnp.einsum('hde,hg->hdge', w, eye).reshape(h * HEAD_DIM, h * HEAD_DIM)


def _inproj_kernel(layer, x_ref, g_ref, w_ref, qg_ref, kg_ref, lb_ref, e_ref,
                   qa_ref, ka_ref, va_ref, ga_ref, qb_ref, lf_ref, kk_ref, ib_ref, gb_ref, uc_ref, gc_ref):
    x = x_ref[...]
    h = x * lax.rsqrt(jnp.mean(x * x, axis=-1, keepdims=True) + EPS) * g_ref[...]
    hb = _bf(h)

    def proj(c0, width):
        return jnp.dot(hb, w_ref[:, c0:c0 + width], preferred_element_type=F32)

    def head_rms(p, g):
        halves = [_seg_sum(p[:, c:c + 256] * p[:, c:c + 256], e_ref) for c in (0, 256)]
        ss = jnp.concatenate(halves, axis=1)
        return p * lax.rsqrt(ss * (1.0 / HEAD_DIM) + EPS) * g

    qa_ref[...] = head_rms(proj(0, D_A), qg_ref[...])
    ka_ref[...] = head_rms(proj(D_A, D_A), kg_ref[...])
    va_ref[...] = proj(2 * D_A, D_A)
    ga_ref[...] = _silu(proj(3 * D_A, D_A))
    c = 4 * D_A
    qb_ref[...] = proj(c, D_B)
    lbp = lb_ref[...]
    lbe = jnp.exp(lbp - jnp.max(lbp, axis=0, keepdims=True))
    lbw = lbe / jnp.sum(lbe, axis=0, keepdims=True)
    lb_cum = lbw[0:1, :]
    for j in range(1, layer + 1):
        lb_cum = lb_cum + lbw[j:j + 1, :]
    lb = lb_cum - lbw[0:1, :]
    fg = lb + (1.0 - lb) * jax.nn.sigmoid(proj(c + D_B, D_B))
    lf_ref[...] = jnp.log(fg)
    kk_ref[...] = 1.0 - fg
    ib_ref[...] = proj(c + 2 * D_B, D_B)
    gb_ref[...] = _silu(proj(c + 3 * D_B, D_B))
    c = c + 4 * D_B
    uc_ref[...] = proj(c, D_C)
    gc_ref[...] = _silu(proj(c + D_C, D_C))


def _inproj(x2d, layer, norm_g, w_bf, qg, kg, hgrn_lb, e256, tm):
    m, d = x2d.shape
    widths = [D_A] * 4 + [D_B] * 5 + [D_C] * 2
    out_shape = [jax.ShapeDtypeStruct((m, w), F32) for w in widths]
    out_specs = [pl.BlockSpec((tm, w), lambda i: (i, 0)) for w in widths]
    return pl.pallas_call(
        functools.partial(_inproj_kernel, layer),
        grid=(m // tm,),
        in_specs=[pl.BlockSpec((tm, d), lambda i: (i, 0)),
                  _const_spec((1, d)), _const_spec(w_bf.shape), _const_spec((1, D_A)), _const_spec((1, D_A)),
                  _const_spec(hgrn_lb.shape), _const_spec((256, 256))],
        out_specs=out_specs,
        out_shape=out_shape,
        compiler_params=pltpu.CompilerParams(dimension_semantics=("arbitrary",), vmem_limit_bytes=VMEM_LIMIT),
        name="inproj",
    )(x2d, norm_g, w_bf, qg, kg, hgrn_lb, e256)


def _outproj_kernel(x_ref, oa_ref, ob_ref, oc_ref, wa_ref, wb_ref, wc_ref, y_ref):
    y_ref[...] = (x_ref[...] + _dot(oa_ref[...], wa_ref[...]) + _dot(ob_ref[...], wb_ref[...])
                  + _dot(oc_ref[...], wc_ref[...]))


def _outproj(x2d, oa, ob, oc, wa, wb, wc, tm):
    m, d = x2d.shape
    row = lambda w: pl.BlockSpec((tm, w), lambda i: (i, 0))
    return pl.pallas_call(
        _outproj_kernel,
        grid=(m // tm,),
        in_specs=[row(d), row(D_A), row(D_B), row(D_C), _const_spec(wa.shape), _const_spec(wb.shape),
                  _const_spec(wc.shape)],
        out_specs=row(d),
        out_shape=jax.ShapeDtypeStruct((m, d), F32),
        compiler_params=pltpu.CompilerParams(dimension_semantics=("arbitrary",), vmem_limit_bytes=VMEM_LIMIT),
        name="outproj",
    )(x2d, oa, ob, oc, wa, wb, wc)


def _moba_prompt_kernel(nb, slopes_ref, q_ref, k_ref, v_ref, g_ref, o_ref, kmean_ref, sel_ref):
    p = pl.program_id(1)
    i = pl.program_id(2)
    blk = MOBA_BLOCK

    @pl.when(i == 0)
    def _():
        kmean_ref[...] = jnp.mean(k_ref[...].reshape(nb, blk, 128), axis=1)

    q2 = q_ref[...]
    lane = lax.broadcasted_iota(jnp.int32, (blk, 128), 1)
    tq = lax.broadcasted_iota(jnp.int32, (blk, blk), 1)
    tk = lax.broadcasted_iota(jnp.int32, (blk, blk), 0)
    d0 = (tq - tk).astype(F32)
    kblk = lax.broadcasted_iota(jnp.int32, (nb, blk), 0)
    heads_out = []
    for hh in range(2):
        in_head = (lane >= HEAD_DIM * hh) & (lane < HEAD_DIM * (hh + 1))
        qm = jnp.where(in_head, q2, 0.0)
        gt = _dot3_nt(kmean_ref[...], qm)
        gt = jnp.where(kblk < i, gt, -jnp.inf)
        cnt = jnp.zeros((nb, blk), jnp.int32)
        for m in range(nb):
            row = gt[m:m + 1, :]
            beats = (row > gt) | ((row == gt) & (m < kblk))
            cnt = cnt + beats.astype(jnp.int32)
        sel_ref[hh] = ((cnt < MOBA_TOPK) & (kblk < i)).astype(F32)
        qs = _bf(qm * (HEAD_DIM ** -0.5))
        slope = slopes_ref[2 * p + hh]

        def step(carry, kb, vb, st):
            m_run, l_run, acc = carry
            m_new = jnp.maximum(m_run, jnp.max(st, axis=0, keepdims=True))
            alpha = jnp.exp(m_run - m_new)
            pt = jnp.exp(st - m_new)
            l_new = alpha * l_run + jnp.sum(pt, axis=0, keepdims=True)
            acc_new = alpha * acc + _dot_tn(vb, pt)
            return m_new, l_new, acc_new

        def past(n, carry):
            start = pl.multiple_of(n * blk, blk)
            kb = k_ref[pl.ds(start, blk), :]
            vb = v_ref[pl.ds(start, blk), :]
            dist = d0 + ((i - n) * blk).astype(F32)
            st = _dot_nt(kb, qs) - slope * dist
            st = jnp.where(sel_ref[hh, pl.ds(n, 1), :] > 0.5, st, NEG)
            return step(carry, kb, vb, st)

        init = (jnp.full((1, blk), NEG, F32), jnp.zeros((1, blk), F32), jnp.zeros((128, blk), F32))
        carry = lax.fori_loop(0, i, past, init)
        start = pl.multiple_of(i * blk, blk)
        kb = k_ref[pl.ds(start, blk), :]
        vb = v_ref[pl.ds(start, blk), :]
        st = _dot_nt(kb, qs) - slope * d0
        st = jnp.where(d0 >= 0.0, st, NEG)
        _, l_fin, acc = step(carry, kb, vb, st)
        heads_out.append(acc / l_fin)
    row = lax.broadcasted_iota(jnp.int32, (128, blk), 0)
    ot = jnp.where(row < HEAD_DIM, heads_out[0], heads_out[1])
    o_ref[...] = ot.T * g_ref[...]


def _moba_prompt(q, k, v, gate, slopes):
    b, t, _ = q.shape
    nb = t // MOBA_BLOCK
    qspec = pl.BlockSpec((None, MOBA_BLOCK, 128), lambda bi, p, i: (bi, i, p))
    kspec = pl.BlockSpec((None, t, 128), lambda bi, p, i: (bi, 0, p))
    return pl.pallas_call(
        functools.partial(_moba_prompt_kernel, nb),
        grid=(b, N_HEADS_A // 2, nb),
        in_specs=[pl.BlockSpec(memory_space=pltpu.SMEM), qspec, kspec, kspec, qspec],
        out_specs=qspec,
        out_shape=jax.ShapeDtypeStruct(q.shape, F32),
        scratch_shapes=[pltpu.VMEM((nb, 128), F32), pltpu.VMEM((2, nb, MOBA_BLOCK), F32)],
        compiler_params=pltpu.CompilerParams(dimension_semantics=("arbitrary",) * 3, vmem_limit_bytes=VMEM_LIMIT),
        name="moba_prompt",
    )(slopes, q, k, v, gate)


def _moba_sample_kernel(n_pages, page, pt_ref, q_ref, kn_ref, vn_ref, g_ref, rowc_ref, heq_ref, tok_ref, hm_ref,
                        *rest):
    kp_refs = rest[:n_pages]
    vp_refs = rest[n_pages:2 * n_pages]
    o_ref = rest[2 * n_pages]
    s_ref = rest[2 * n_pages + 1]
    del pt_ref
    t_new = q_ref.shape[0]
    nh = N_HEADS_A
    rows = nh * t_new
    lanes = page * nh
    pages_per_blk = MOBA_BLOCK // page
    nblk = n_pages // pages_per_blk
    q = q_ref[...]
    qht = jnp.concatenate([q[:, h * HEAD_DIM:(h + 1) * HEAD_DIM] for h in range(nh)], axis=0)
    qs = _bf(qht * (HEAD_DIM ** -0.5))
    slope = rowc_ref[:, 0:1]
    qpos = rowc_ref[:, 1:2]
    heq = heq_ref[...]
    tok = tok_ref[...]

    ksum = [None] * nblk
    for j in range(n_pages):
        kp = kp_refs[j][...]
        pj = jnp.sum(kp, axis=0)
        n = j // pages_per_blk
        ksum[n] = pj if ksum[n] is None else ksum[n] + pj
        st = _dot_nt(qs, kp.reshape(lanes, HEAD_DIM))
        dist = (qpos - float(j * page)) - tok
        s_ref[:, j * lanes:(j + 1) * lanes] = st - slope * dist

    gates = []
    for n in range(nblk):
        km = ksum[n] * (1.0 / MOBA_BLOCK)
        kmr = jnp.concatenate([jnp.broadcast_to(km[h:h + 1, :], (t_new, HEAD_DIM)) for h in range(nh)], axis=0)
        gates.append(jnp.sum(qht * kmr, axis=1, keepdims=True))
    sels = []
    for n in range(nblk):
        cnt = jnp.zeros((rows, 1), jnp.int32)
        for m in range(nblk):
            if m == n:
                continue
            beats = (gates[m] > gates[n]) | ((gates[m] == gates[n]) & (m < n))
            cnt = cnt + beats.astype(jnp.int32)
        sels.append((cnt < MOBA_TOPK).astype(F32))

    hm = hm_ref[...]
    qblk = jnp.concatenate([q] * nh, axis=0) * hm
    s_own = _dot_nt(qblk * (HEAD_DIM ** -0.5), kn_ref[...])
    trow = lax.broadcasted_iota(jnp.int32, (rows, t_new), 0) % t_new
    tcol = lax.broadcasted_iota(jnp.int32, (rows, t_new), 1)
    dist_own = (trow - tcol).astype(F32)
    s_own = jnp.where(dist_own >= 0.0, s_own - slope * dist_own, NEG)

    mvec = jnp.full((rows, lanes), NEG, F32)
    for j in range(n_pages):
        keep = heq * sels[j // pages_per_blk]
        st = jnp.where(keep > 0.5, s_ref[:, j * lanes:(j + 1) * lanes], NEG)
        s_ref[:, j * lanes:(j + 1) * lanes] = st
        mvec = jnp.maximum(mvec, st)
    m_row = jnp.maximum(jnp.max(mvec, axis=1, keepdims=True), jnp.max(s_own, axis=1, keepdims=True))

    p_own = jnp.exp(s_own - m_row)
    lvec = jnp.zeros((rows, lanes), F32)
    acc = jnp.zeros((rows, HEAD_DIM), F32)
    for j in range(n_pages):
        pj = jnp.exp(s_ref[:, j * lanes:(j + 1) * lanes] - m_row)
        lvec = lvec + pj
        acc = acc + _dot(pj, vp_refs[j][...].reshape(lanes, HEAD_DIM))
    l_row = jnp.sum(lvec, axis=1, keepdims=True) + jnp.sum(p_own, axis=1, keepdims=True)
    own = _dot(p_own, vn_ref[...])
    inv = 1.0 / l_row
    for h in range(nh):
        r0 = h * t_new
        oh = (acc[r0:r0 + t_new, :] + own[r0:r0 + t_new, h * HEAD_DIM:(h + 1) * HEAD_DIM]) * inv[r0:r0 + t_new, :]
        o_ref[:, h * HEAD_DIM:(h + 1) * HEAD_DIM] = oh * g_ref[:, h * HEAD_DIM:(h + 1) * HEAD_DIM]


def _moba_sample(q, k_new, v_new, gate, cache_k, cache_v, pt_flat, layer, consts, n_seq, t_new, n_pages):
    page = cache_k.shape[2]
    rowc, heq, tok, hm = consts
    rows = N_HEADS_A * t_new
    lanes = page * N_HEADS_A
    tspec = pl.BlockSpec((t_new, D_A), lambda b, pt: (b, 0))

    def page_spec(j):
        return pl.BlockSpec((None, None, page, N_HEADS_A, HEAD_DIM),
                            lambda b, pt, j=j: (pt[b * n_pages + j], layer, 0, 0, 0))

    cspec = lambda a: pl.BlockSpec(a.shape, lambda b, pt: (0,) * a.ndim)
    grid_spec = pltpu.PrefetchScalarGridSpec(
        num_scalar_prefetch=1,
        grid=(n_seq,),
        in_specs=[tspec, tspec, tspec, tspec, cspec(rowc), cspec(heq), cspec(tok), cspec(hm)]
        + [page_spec(j) for j in range(n_pages)] + [page_spec(j) for j in range(n_pages)],
        out_specs=tspec,
        scratch_shapes=[pltpu.VMEM((rows, n_pages * lanes), F32)],
    )
    return pl.pallas_call(
        functools.partial(_moba_sample_kernel, n_pages, page),
        grid_spec=grid_spec,
        out_shape=jax.ShapeDtypeStruct(q.shape, F32),
        compiler_params=pltpu.CompilerParams(dimension_semantics=("arbitrary",), vmem_limit_bytes=VMEM_LIMIT),
        name="moba_sample",
    )(pt_flat, q, k_new, v_new, gate, rowc, heq, tok, hm, *([cache_k] * n_pages), *([cache_v] * n_pages))


def _moba_sample_consts(t_new, page, past_len):
    nh = N_HEADS_A
    rows = nh * t_new
    lanes = page * nh
    r = np.arange(rows)
    slopes = 2.0 ** (-8.0 * (np.arange(nh) + 1) / nh)
    rowc = np.zeros((rows, 2), np.float32)
    rowc[:, 0] = slopes[r // t_new]
    rowc[:, 1] = past_len + (r % t_new)
    ln = np.arange(lanes)
    heq = ((ln[None, :] % nh) == (r[:, None] // t_new)).astype(np.float32)
    tok = np.broadcast_to((ln // nh).astype(np.float32)[None, :], (rows, lanes)).copy()
    c = np.arange(nh * HEAD_DIM)
    hm = ((c[None, :] // HEAD_DIM) == (r[:, None] // t_new)).astype(np.float32)
    return jnp.asarray(rowc), jnp.asarray(heq), jnp.asarray(tok), jnp.asarray(hm)


def _hgrn_consts(c):
    levels = int(round(math.log2(c)))
    t = np.arange(c)[:, None]
    u = np.arange(c)[None, :]
    mats = [u <= t, u > t]
    masks = [t == u]
    for j in range(1, levels + 1):
        p = 2 ** j
        hlf = p // 2
        mid = (t // p) * p + hlf
        upper = (t % p) >= hlf
        mats.append(upper & (u >= mid) & (u <= t))
        mats.append((~upper) & (u > t) & (u <= mid - 1))
        masks.append(((t // p) == (u // p)) & upper & ((u % p) < hlf))
    w_all = jnp.asarray(np.concatenate(mats, axis=0), dtype=BF16)
    mk = jnp.asarray(np.stack(masks), dtype=F32)
    return w_all, mk


def _hgrn_kernel(c, levels, q_ref, lf_ref, kk_ref, v_ref, g_ref, s0_ref, ng_ref, w_ref, mk_ref, e_ref,
                 o_ref, sn_ref, sbd_ref):
    ci = pl.program_id(1)
    nc = pl.num_programs(1)
    nh = N_HEADS_B
    width = nh * HEAD_DIM

    @pl.when(ci == 0)
    def _():
        sbd_ref[...] = jnp.zeros((width, width), F32)
        for h in range(nh):
            sbd_ref[h * HEAD_DIM:(h + 1) * HEAD_DIM, h * HEAD_DIM:(h + 1) * HEAD_DIM] = s0_ref[h]

    lf = lf_ref[...]
    e_all = jnp.exp(_dot_sel(w_ref[...], lf))
    e_b = e_all[0:c]
    e_end = e_all[c:2 * c]
    q = q_ref[...]
    k = kk_ref[...]
    v = v_ref[...]
    vb = _bf(v)
    sbd = sbd_ref[...]
    o = _dot(q * e_b, sbd)
    lane = lax.broadcasted_iota(jnp.int32, (c, width), 1)
    qk_levels = [(q, k)]
    for j in range(1, levels + 1):
        qk_levels.append((q * e_all[2 * j * c:(2 * j + 1) * c], k * e_all[(2 * j + 1) * c:(2 * j + 2) * c]))
    for h in range(nh):
        in_head = (lane >= h * HEAD_DIM) & (lane < (h + 1) * HEAD_DIM)
        a = jnp.zeros((c, c), F32)
        for j, (qj, kj) in enumerate(qk_levels):
            a = a + _dot_nt(jnp.where(in_head, qj, 0.0), kj) * mk_ref[j]
        o = o + jnp.where(in_head, jnp.dot(_bf(a), vb, preferred_element_type=F32), 0.0)

    ones = jnp.ones((c, 128), BF16)
    hi, mid, lo = _split3(lf)
    dn = (((0,), (0,)), ((), ()))
    colsum = (lax.dot_general(hi, ones, dn, preferred_element_type=F32)
              + lax.dot_general(mid, ones, dn, preferred_element_type=F32)
              + lax.dot_general(lo, ones, dn, preferred_element_type=F32))
    decay = jnp.exp(colsum)
    decay = jnp.concatenate([decay] * (width // 128), axis=1)
    r = lax.broadcasted_iota(jnp.int32, (width, width), 0) // HEAD_DIM
    cc = lax.broadcasted_iota(jnp.int32, (width, width), 1) // HEAD_DIM
    s_new = sbd * decay + jnp.where(r == cc, _dot_tn(k * e_end, v), 0.0)
    sbd_ref[...] = s_new

    ss = _seg_sum(o * o, e_ref)
    o_ref[...] = o * lax.rsqrt(ss * (1.0 / HEAD_DIM) + EPS) * ng_ref[...] * g_ref[...]

    @pl.when(ci == nc - 1)
    def _():
        for h in range(nh):
            sn_ref[h] = s_new[h * HEAD_DIM:(h + 1) * HEAD_DIM, h * HEAD_DIM:(h + 1) * HEAD_DIM]


def _hgrn(q, lf, kk, v, gate, s0, ng, consts, e256, c):
    b, t, width = q.shape
    w_all, mk = consts
    levels = mk.shape[0] - 1
    tok = pl.BlockSpec((None, c, width), lambda bi, ci: (bi, ci, 0))
    st = pl.BlockSpec((None, N_HEADS_B, HEAD_DIM, HEAD_DIM), lambda bi, ci: (bi, 0, 0, 0))
    return pl.pallas_call(
        functools.partial(_hgrn_kernel, c, levels),
        grid=(b, t // c),
        in_specs=[tok, tok, tok, tok, tok, st, _const_spec((1, width)), _const_spec(w_all.shape),
                  _const_spec(mk.shape), _const_spec((256, 256))],
        out_specs=[tok, st],
        out_shape=[jax.ShapeDtypeStruct(q.shape, F32), jax.ShapeDtypeStruct(s0.shape, F32)],
        scratch_shapes=[pltpu.VMEM((width, width), F32)],
        compiler_params=pltpu.CompilerParams(dimension_semantics=("arbitrary", "arbitrary"),
                                             vmem_limit_bytes=VMEM_LIMIT),
        name="hgrn",
    )(q, lf, kk, v, gate, s0, ng, w_all, mk, e256)


def _mlstm_consts(c):
    t = np.arange(c)[:, None]
    u = np.arange(c)[None, :]
    return jnp.asarray(u <= t, dtype=BF16)


def _mlstm_kernel(c, uc_ref, g_ref, cv0_ref, c0_ref, n0_ref, m0_ref, cw_ref, cb_ref, wq_ref, wk_ref, wv_ref,
                  wi_ref, wf_ref, bi_ref, bf_ref, skip_ref, ng_ref, tri_ref, e_ref,
                  o_ref, cn_ref, nn_ref, mn_ref, cvn_ref,
                  uext_ref, cbd_ref, n_ref, m_ref):
    ci = pl.program_id(1)
    nc = pl.num_programs(1)
    nh = N_HEADS_C
    width = nh * HEAD_DIM

    @pl.when(ci == 0)
    def _():
        uext_ref[0:8, :] = cv0_ref[...]
        cbd_ref[...] = jnp.zeros((width, width), F32)
        for h in range(nh):
            cbd_ref[h * HEAD_DIM:(h + 1) * HEAD_DIM, h * HEAD_DIM:(h + 1) * HEAD_DIM] = c0_ref[h]
        n_ref[...] = n0_ref[...]
        m_ref[...] = m0_ref[...]

    uc = uc_ref[...]
    uext_ref[8:8 + c, :] = uc
    conv = cb_ref[...]
    for j in range(CONV_W):
        conv = conv + cw_ref[j:j + 1, :] * uext_ref[5 + j:5 + j + c, :]
    tail = uext_ref[c:c + 8, :]
    uext_ref[0:8, :] = tail
    uconv = _silu(conv)

    qm = _dot(uconv, wq_ref[...])
    km = _dot(uconv, wk_ref[...])
    vm = _dot(uc, wv_ref[...])
    i_raw = _dot3(qm, wi_ref[0]) + _dot3(km, wi_ref[1]) + _dot3(vm, wi_ref[2]) + bi_ref[...]
    f_pre = _dot3(qm, wf_ref[0]) + _dot3(km, wf_ref[1]) + _dot3(vm, wf_ref[2]) + bf_ref[...]
    logf = _log_sigmoid(f_pre)
    bcum = _dot_sel(tri_ref[...], logf)
    a = i_raw - bcum
    rowi = lax.broadcasted_iota(jnp.int32, (c, GATE_LANES), 0)
    s = 1
    while s < c:
        a = jnp.maximum(a, jnp.where(rowi >= s, pltpu.roll(a, s, 0), -jnp.inf))
        s *= 2
    m0 = m_ref[...]
    m_t = bcum + jnp.maximum(m0, a)
    g_in = jnp.exp(bcum + m0 - m_t)
    bm = bcum - m_t
    ib_t = (i_raw - bcum).T
    m_end = m_t[c - 1:c, :]
    b_end = bcum[c - 1:c, :]
    w_tok = jnp.exp((b_end - bcum) + i_raw - m_end)
    g_end = jnp.exp(b_end + m0 - m_end)

    ks = km * (HEAD_DIM ** -0.5)
    ksb = _bf(ks)
    vmb = _bf(vm)
    lane = lax.broadcasted_iota(jnp.int32, (c, width), 1)
    trow = lax.broadcasted_iota(jnp.int32, (c, c), 0)
    tcol = lax.broadcasted_iota(jnp.int32, (c, c), 1)
    causal = tcol <= trow
    g256 = _spread_heads(g_in, c, nh)
    cbd = cbd_ref[...]
    n0 = n_ref[...]
    num = g256 * _dot(qm, cbd)
    qn = qm * n0
    den_cols = jnp.zeros((c, GATE_LANES), F32)
    lane_g = lax.broadcasted_iota(jnp.int32, (c, GATE_LANES), 1)
    for h in range(nh):
        in_head = (lane >= h * HEAD_DIM) & (lane < (h + 1) * HEAD_DIM)
        expo = bm[:, h:h + 1] + ib_t[h:h + 1, :]
        dmat = jnp.exp(jnp.where(causal, expo, NEG))
        qk = _dot_nt(jnp.where(in_head, qm, 0.0), ksb) * dmat
        num = num + jnp.where(in_head, jnp.dot(_bf(qk), vmb, preferred_element_type=F32), 0.0)
        den_h = (g_in[:, h:h + 1] * jnp.sum(jnp.where(in_head, qn, 0.0), axis=1, keepdims=True)
                 + jnp.sum(qk, axis=1, keepdims=True))
        den_cols = jnp.where(lane_g == h, den_h, den_cols)
    denom = jnp.maximum(jnp.abs(den_cols), jnp.exp(-m_t))
    hval = num / _spread_heads(denom, c, nh)

    w256 = _spread_heads(w_tok, c, nh)
    gend256 = _spread_heads(g_end, 1, nh)
    r = lax.broadcasted_iota(jnp.int32, (width, width), 0) // HEAD_DIM
    cc = lax.broadcasted_iota(jnp.int32, (width, width), 1) // HEAD_DIM
    kw = ks * w256
    c_new = cbd * gend256 + jnp.where(r == cc, _dot_tn(kw, vm), 0.0)
    n_new = gend256 * n0 + jnp.sum(kw, axis=0, keepdims=True)
    cbd_ref[...] = c_new
    n_ref[...] = n_new
    m_ref[...] = m_end

    mean = _seg_sum(hval, e_ref) * (1.0 / HEAD_DIM)
    xc = hval - mean
    var = _seg_sum(xc * xc, e_ref) * (1.0 / HEAD_DIM)
    hc = xc * lax.rsqrt(var + EPS) * ng_ref[...]
    o_ref[...] = (hc + skip_ref[...] * uconv) * g_ref[...]

    @pl.when(ci == nc - 1)
    def _():
        for h in range(nh):
            cn_ref[h] = c_new[h * HEAD_DIM:(h + 1) * HEAD_DIM, h * HEAD_DIM:(h + 1) * HEAD_DIM]
        nn_ref[...] = n_new
        mn_ref[...] = m_end
        cvn_ref[...] = tail


def _mlstm(uc, gate, cv0, c0, n0, m0, lw, consts, e256, c):
    b, t, width = uc.shape
    tri = consts
    tok = pl.BlockSpec((None, c, width), lambda bi, ci: (bi, ci, 0))
    per_b = lambda shp: pl.BlockSpec((None,) + shp, lambda bi, ci: (bi,) + (0,) * len(shp))
    weights = [lw['conv_w'], lw['conv_b'], lw['wq'], lw['wk'], lw['wv'], lw['wi'], lw['wf'], lw['bi'], lw['bf'],
               lw['skip'], lw['mng'], tri, e256]
    out_shape = [jax.ShapeDtypeStruct(uc.shape, F32), jax.ShapeDtypeStruct(c0.shape, F32),
                 jax.ShapeDtypeStruct(n0.shape, F32), jax.ShapeDtypeStruct(m0.shape, F32),
                 jax.ShapeDtypeStruct(cv0.shape, F32)]
    return pl.pallas_call(
        functools.partial(_mlstm_kernel, c),
        grid=(b, t // c),
        in_specs=[tok, tok, per_b((8, width)), per_b((N_HEADS_C, HEAD_DIM, HEAD_DIM)), per_b((1, width)),
                  per_b((1, GATE_LANES))] + [_const_spec(w.shape) for w in weights],
        out_specs=[tok, per_b((N_HEADS_C, HEAD_DIM, HEAD_DIM)), per_b((1, width)), per_b((1, GATE_LANES)),
                   per_b((8, width))],
        out_shape=out_shape,
        scratch_shapes=[pltpu.VMEM((c + 8, width), F32), pltpu.VMEM((width, width), F32),
                        pltpu.VMEM((1, width), F32), pltpu.VMEM((1, GATE_LANES), F32)],
        compiler_params=pltpu.CompilerParams(dimension_semantics=("arbitrary", "arbitrary"),
                                             vmem_limit_bytes=VMEM_LIMIT),
        name="mlstm",
    )(uc, gate, cv0, c0, n0, m0, *weights)


def _gate_weights(w):
    w3 = w.reshape(3, D_C, N_HEADS_C)
    return jnp.pad(w3, ((0, 0), (0, 0), (0, GATE_LANES - N_HEADS_C)))


def _pad_lanes(v):
    return jnp.pad(v, ((0, 0),) * (v.ndim - 1) + ((0, GATE_LANES - v.shape[-1]),))


def _run_group(x, layer, lw, shared, attn_fn, s0, c0, n0, m0, cv0, chunk, tm):
    b, t, d = x.shape
    e256 = shared['e256']
    x2d = x.reshape(b * t, d)
    (qa, ka, va, ga, qb, lf, kk, ib, gb, uc, gc) = _inproj(
        x2d, layer, lw['norm_g'], lw['w_in'], lw['qg'], lw['kg'], shared['hgrn_lb'], e256, tm)
    r3 = lambda a: a.reshape(b, t, a.shape[-1])
    oa = attn_fn(qa, ka, va, ga)
    ob, s_new = _hgrn(r3(qb), r3(lf), r3(kk), r3(ib), r3(gb), s0, lw['hng'], shared['hgrn_consts'][chunk], e256,
                      chunk)
    oc, c_new, n_new, m_new, cv_new = _mlstm(r3(uc), r3(gc), cv0, c0, n0, m0, lw, shared['mlstm_consts'][chunk],
                                             e256, chunk)
    y = _outproj(x2d, oa.reshape(b * t, D_A), ob.reshape(b * t, D_B), oc.reshape(b * t, D_C),
                 lw['wo_a'], lw['wo_b'], lw['wo_c'], tm)
    return (y.reshape(b, t, d), ka.reshape(b, t, N_HEADS_A, HEAD_DIM), va.reshape(b, t, N_HEADS_A, HEAD_DIM),
            s_new, c_new, n_new.reshape(b, N_HEADS_C, HEAD_DIM), m_new[:, 0, :N_HEADS_C],
            cv_new[:, 8 - (CONV_W - 1):, :])


def kernel(x_prompt, x_sample, cache_k, cache_v, page_table, state_hgrn, state_mlstm_c, state_mlstm_n,
           state_mlstm_m, state_mlstm_conv, norm_g, w_in, q_norm_g, k_norm_g, hgrn_lb, hgrn_norm_g,
           mlstm_conv_w, mlstm_conv_b, mlstm_wq, mlstm_wk, mlstm_wv, mlstm_w_ig, mlstm_b_ig, mlstm_w_fg,
           mlstm_b_fg, mlstm_skip, mlstm_norm_g, w_out):
    depth = w_in.shape[0]
    bp, tp, _ = x_prompt.shape
    bd, td, _ = x_sample.shape
    n_pages = page_table.shape[1]
    page = cache_k.shape[2]
    past_len = n_pages * page
    chunk_p = min(tp, 256)
    chunk_d = td
    shared = {
        'e256': _head_block_ones(),
        'hgrn_lb': hgrn_lb.astype(F32),
        'hgrn_consts': {c: _hgrn_consts(c) for c in {chunk_p, chunk_d}},
        'mlstm_consts': {c: _mlstm_consts(c) for c in {chunk_p, chunk_d}},
    }
    slopes = jnp.asarray(2.0 ** (-8.0 * (np.arange(N_HEADS_A) + 1) / N_HEADS_A), dtype=F32)
    sample_consts = _moba_sample_consts(td, page, past_len)
    pt_flat = page_table.reshape(-1).astype(jnp.int32)

    zeros_p = dict(
        s0=jnp.zeros((bp, N_HEADS_B, HEAD_DIM, HEAD_DIM), F32),
        c0=jnp.zeros((bp, N_HEADS_C, HEAD_DIM, HEAD_DIM), F32),
        n0=jnp.zeros((bp, 1, D_C), F32),
        m0=jnp.zeros((bp, 1, GATE_LANES), F32),
        cv0=jnp.zeros((bp, 8, D_C), F32),
    )
    yp, ys = x_prompt, x_sample
    outs_p = [[] for _ in range(7)]
    outs_s = [[] for _ in range(7)]
    for l in range(depth):
        lw = {
            'norm_g': norm_g[l][None, :],
            'w_in': w_in[l].astype(BF16),
            'qg': jnp.tile(q_norm_g[l], N_HEADS_A)[None, :],
            'kg': jnp.tile(k_norm_g[l], N_HEADS_A)[None, :],
            'hng': hgrn_norm_g[l][None, :],
            'conv_w': mlstm_conv_w[l],
            'conv_b': mlstm_conv_b[l][None, :],
            'wq': _block_diag(mlstm_wq[l]).astype(BF16),
            'wk': _block_diag(mlstm_wk[l]).astype(BF16),
            'wv': _block_diag(mlstm_wv[l]).astype(BF16),
            'wi': _gate_weights(mlstm_w_ig[l]),
            'wf': _gate_weights(mlstm_w_fg[l]),
            'bi': _pad_lanes(mlstm_b_ig[l][None, :]),
            'bf': _pad_lanes(mlstm_b_fg[l][None, :]),
            'skip': mlstm_skip[l][None, :],
            'mng': mlstm_norm_g[l][None, :],
            'wo_a': w_out[l][:D_A].astype(BF16),
            'wo_b': w_out[l][D_A:D_A + D_B].astype(BF16),
            'wo_c': w_out[l][D_A + D_B:].astype(BF16),
        }

        def attn_prompt(qa, ka, va, ga):
            r3 = lambda a: a.reshape(bp, tp, D_A)
            return _moba_prompt(r3(qa), r3(ka), r3(va), r3(ga), slopes)

        def attn_sample(qa, ka, va, ga, l=l):
            return _moba_sample(qa, ka, va, ga, cache_k, cache_v, pt_flat, l, sample_consts, bd, td, n_pages)

        res_p = _run_group(yp, l, lw, shared, attn_prompt, zeros_p['s0'], zeros_p['c0'], zeros_p['n0'],
                           zeros_p['m0'], zeros_p['cv0'], chunk_p, 256)
        cv0_s = jnp.pad(state_mlstm_conv[l], ((0, 0), (8 - (CONV_W - 1), 0), (0, 0)))
        res_s = _run_group(ys, l, lw, shared, attn_sample, state_hgrn[l], state_mlstm_c[l],
                           state_mlstm_n[l].reshape(bd, 1, D_C), _pad_lanes(state_mlstm_m[l])[:, None, :],
                           cv0_s, chunk_d, 256)
        yp, ys = res_p[0], res_s[0]
        for acc, a in zip(outs_p, res_p[1:]):
            acc.append(a)
        for acc, a in zip(outs_s, res_s[1:]):
            acc.append(a)

    st = lambda lst, ax: jnp.stack(lst, axis=ax)
    return (yp, ys, st(outs_p[0], 1), st(outs_p[1], 1), st(outs_s[0], 1), st(outs_s[1], 1),
            st(outs_p[2], 0), st(outs_s[2], 0), st(outs_p[3], 0), st(outs_s[3], 0),
            st(outs_p[4], 0), st(outs_s[4], 0), st(outs_p[5], 0), st(outs_s[5], 0),
            st(outs_p[6], 0), st(outs_s[6], 0))
```

```python
import functools
import math

import numpy as np
import jax
import jax.numpy as jnp
from jax import lax
from jax.experimental import pallas as pl
from jax.experimental.pallas import tpu as pltpu

F32 = jnp.float32
BF16 = jnp.bfloat16

HEAD_DIM = 64
N_HEADS_A = 8
N_HEADS_B = 4
N_HEADS_C = 4
D_A = N_HEADS_A * HEAD_DIM
D_B = N_HEADS_B * HEAD_DIM
D_C = N_HEADS_C * HEAD_DIM
MOBA_BLOCK = 256
MOBA_TOPK = 3
CONV_W = 4
EPS = 1e-6
NEG = -1e30
GATE_LANES = 128
LANES = 128
SUBLANES = 8
VMEM_LIMIT = 56 * 1024 * 1024
QK_SCALE = HEAD_DIM ** -0.5


def _bf(x):
    return x.astype(BF16)


def _dot(a, b):
    return jnp.dot(_bf(a), _bf(b), preferred_element_type=F32)


def _dot_nt(a, b):
    return lax.dot_general(_bf(a), _bf(b), (((1,), (1,)), ((), ())), preferred_element_type=F32)


def _dot_tn(a, b):
    return lax.dot_general(_bf(a), _bf(b), (((0,), (0,)), ((), ())), preferred_element_type=F32)


def _split2(x):
    hi = _bf(x)
    lo = _bf(x - hi.astype(F32))
    return hi, lo


def _split3(x):
    hi = _bf(x)
    r = x - hi.astype(F32)
    mid = _bf(r)
    lo = _bf(r - mid.astype(F32))
    return hi, mid, lo


def _dot_sel(w01, x, parts=3):
    pieces = _split3(x)[:parts]
    out = jnp.dot(w01, pieces[0], preferred_element_type=F32)
    for p in pieces[1:]:
        out = out + jnp.dot(w01, p, preferred_element_type=F32)
    return out


def _dot3(a, b):
    ah, al = _split2(a)
    bh, bl = _split2(b)
    return (jnp.dot(ah, bh, preferred_element_type=F32) + jnp.dot(al, bh, preferred_element_type=F32)
            + jnp.dot(ah, bl, preferred_element_type=F32))


def _dot3_tn(a, b):
    ah, al = _split2(a)
    bh, bl = _split2(b)
    dn = (((0,), (0,)), ((), ()))
    return (lax.dot_general(ah, bh, dn, preferred_element_type=F32)
            + lax.dot_general(al, bh, dn, preferred_element_type=F32)
            + lax.dot_general(ah, bl, dn, preferred_element_type=F32))


def _seg_sum(x, e_ref):
    hi, lo = _split2(x)
    e = e_ref[...]
    return jnp.dot(hi, e, preferred_element_type=F32) + jnp.dot(lo, e, preferred_element_type=F32)


def _spread_heads(x, rows, n_heads):
    lane = lax.broadcasted_iota(jnp.int32, (rows, n_heads * HEAD_DIM), 1)
    out = jnp.zeros((rows, n_heads * HEAD_DIM), F32)
    for h in range(n_heads):
        in_head = (lane >= h * HEAD_DIM) & (lane < (h + 1) * HEAD_DIM)
        out = jnp.where(in_head, x[:, h:h + 1], out)
    return out


def _silu(x):
    return x * jax.nn.sigmoid(x)


def _log_sigmoid(x):
    return jnp.minimum(x, 0.0) - jnp.log(1.0 + jnp.exp(-jnp.abs(x)))


def _topk_rows(g, n_rows, limit):
    rid = lax.broadcasted_iota(jnp.int32, g.shape, 0)
    g = jnp.where(rid < limit, g, -jnp.inf)
    cnt = jnp.zeros(g.shape, jnp.int32)
    for m in range(n_rows):
        row = g[m:m + 1, :]
        beats = (row > g) | ((row == g) & (m < rid))
        cnt = cnt + beats.astype(jnp.int32)
    return ((cnt < MOBA_TOPK) & (rid < limit)).astype(F32)


def _const_spec(shape):
    nd = len(shape)
    return pl.BlockSpec(shape, lambda *_: (0,) * nd)


def _head_block_ones():
    r = np.arange(256)
    return jnp.asarray((r[:, None] // HEAD_DIM) == (r[None, :] // HEAD_DIM), dtype=BF16)


def _block_diag(w):
    h = w.shape[0]
    eye = jnp.eye(h, dtype=w.dtype)
    return jnp.einsum('hde,hg->hdge', w, eye).reshape(h * HEAD_DIM, h * HEAD_DIM)


def _params(*sem):
    return pltpu.CompilerParams(dimension_semantics=sem, vmem_limit_bytes=VMEM_LIMIT)


def _inproj_kernel(layer, kv_transposed, x_ref, g_ref, w_ref, qg_ref, kg_ref, lb_ref, e_ref,
                   qa_ref, ka_ref, va_ref, ga_ref, qb_ref, lf_ref, kk_ref, ib_ref, gb_ref, uc_ref, gc_ref):
    x = x_ref[...]
    h = x * lax.rsqrt(jnp.mean(x * x, axis=-1, keepdims=True) + EPS) * g_ref[...]
    hb = _bf(h)

    def proj(c0, width):
        return jnp.dot(hb, w_ref[:, c0:c0 + width], preferred_element_type=F32)

    def head_rms(p, g):
        halves = [_seg_sum(p[:, c:c + 256] * p[:, c:c + 256], e_ref) for c in (0, 256)]
        ss = jnp.concatenate(halves, axis=1)
        return p * lax.rsqrt(ss * (1.0 / HEAD_DIM) + EPS) * g

    qa_ref[...] = head_rms(proj(0, D_A), qg_ref[...])
    ka = head_rms(proj(D_A, D_A), kg_ref[...])
    va = proj(2 * D_A, D_A)
    if kv_transposed:
        ka_ref[...] = ka.T
        va_ref[...] = va.T
    else:
        ka_ref[...] = ka
        va_ref[...] = va
    ga_ref[...] = _silu(proj(3 * D_A, D_A))
    c = 4 * D_A
    qb_ref[...] = proj(c, D_B)
    lbp = lb_ref[...]
    lbe = jnp.exp(lbp - jnp.max(lbp, axis=0, keepdims=True))
    lbw = lbe / jnp.sum(lbe, axis=0, keepdims=True)
    lb_cum = lbw[0:1, :]
    for j in range(1, layer + 1):
        lb_cum = lb_cum + lbw[j:j + 1, :]
    lb = lb_cum - lbw[0:1, :]
    fg = lb + (1.0 - lb) * jax.nn.sigmoid(proj(c + D_B, D_B))
    lf_ref[...] = jnp.log(fg)
    kk_ref[...] = 1.0 - fg
    ib_ref[...] = proj(c + 2 * D_B, D_B)
    gb_ref[...] = _silu(proj(c + 3 * D_B, D_B))
    c = c + 4 * D_B
    uc_ref[...] = proj(c, D_C)
    gc_ref[...] = _silu(proj(c + D_C, D_C))


def _inproj(x2d, layer, lw, shared, tm, seq_len=None):
    m, d = x2d.shape
    kv_transposed = seq_len is not None
    widths = [D_A] * 4 + [D_B] * 5 + [D_C] * 2
    out_shape = [jax.ShapeDtypeStruct((m, w), F32) for w in widths]
    out_specs = [pl.BlockSpec((tm, w), lambda i: (i, 0)) for w in widths]
    if kv_transposed:
        tiles = seq_len // tm
        for idx in (1, 2):
            out_shape[idx] = jax.ShapeDtypeStruct((m // seq_len, D_A, seq_len), F32)
            out_specs[idx] = pl.BlockSpec((None, D_A, tm), lambda i: (i // tiles, 0, i % tiles))
    w_bf = lw['w_in']
    return pl.pallas_call(
        functools.partial(_inproj_kernel, layer, kv_transposed),
        grid=(m // tm,),
        in_specs=[pl.BlockSpec((tm, d), lambda i: (i, 0)),
                  _const_spec((1, d)), _const_spec(w_bf.shape), _const_spec((1, D_A)), _const_spec((1, D_A)),
                  _const_spec(shared['hgrn_lb'].shape), _const_spec((256, 256))],
        out_specs=out_specs,
        out_shape=out_shape,
        compiler_params=_params("arbitrary"),
        name="inproj",
    )(x2d, lw['norm_g'], w_bf, lw['qg'], lw['kg'], shared['hgrn_lb'], shared['e256'])


def _outproj_kernel(x_ref, oa_ref, ob_ref, oc_ref, wa_ref, wb_ref, wc_ref, y_ref):
    y_ref[...] = (x_ref[...] + _dot(oa_ref[...], wa_ref[...]) + _dot(ob_ref[...], wb_ref[...])
                  + _dot(oc_ref[...], wc_ref[...]))


def _outproj(x2d, oa, ob, oc, lw, tm):
    m, d = x2d.shape
    row = lambda w: pl.BlockSpec((tm, w), lambda i: (i, 0))
    wa, wb, wc = lw['wo_a'], lw['wo_b'], lw['wo_c']
    return pl.pallas_call(
        _outproj_kernel,
        grid=(m // tm,),
        in_specs=[row(d), row(D_A), row(D_B), row(D_C), _const_spec(wa.shape), _const_spec(wb.shape),
                  _const_spec(wc.shape)],
        out_specs=row(d),
        out_shape=jax.ShapeDtypeStruct((m, d), F32),
        compiler_params=_params("arbitrary"),
        name="outproj",
    )(x2d, oa, ob, oc, wa, wb, wc)


HEADS_PER_STEP = 4


def _moba_prompt_kernel(nb, slopes_ref, q_ref, kt_ref, vt_ref, g_ref, o_ref, kmt_ref, sel_ref):
    g = pl.program_id(1)
    i = pl.program_id(2)
    blk = MOBA_BLOCK
    nh = HEADS_PER_STEP
    width = nh * HEAD_DIM

    @pl.when(i == 0)
    def _():
        lane = lax.broadcasted_iota(jnp.int32, (width, LANES), 1)
        km = jnp.zeros((width, LANES), F32)
        for n in range(nb):
            col = jnp.sum(kt_ref[:, n * blk:(n + 1) * blk], axis=1, keepdims=True) * (1.0 / blk)
            km = jnp.where(lane == n, col, km)
        kmt_ref[...] = km

    qt = q_ref[...].T
    qts = _bf(qt * QK_SCALE)
    tq = lax.broadcasted_iota(jnp.int32, (blk, blk), 1)
    tk = lax.broadcasted_iota(jnp.int32, (blk, blk), 0)
    d0 = (tq - tk).astype(F32)
    slopes = []
    for h in range(nh):
        r0 = h * HEAD_DIM
        gt = _dot3_tn(kmt_ref[r0:r0 + HEAD_DIM, :], qt[r0:r0 + HEAD_DIM, :])[0:nb]
        sel_ref[h] = _topk_rows(gt, nb, i)
        slopes.append(slopes_ref[nh * g + h])

    def head_step(h, carry, start, st):
        m_run, l_run, acc = carry
        r0 = h * HEAD_DIM
        m_new = jnp.maximum(m_run, jnp.max(st, axis=0, keepdims=True))
        alpha = jnp.exp(m_run - m_new)
        pt = jnp.exp(st - m_new)
        l_new = alpha * l_run + jnp.sum(pt, axis=0, keepdims=True)
        vb = vt_ref[r0:r0 + HEAD_DIM, pl.ds(start, blk)]
        acc_new = alpha * acc + _dot(vb, pt)
        return m_new, l_new, acc_new

    def scores(h, start):
        r0 = h * HEAD_DIM
        kb = kt_ref[r0:r0 + HEAD_DIM, pl.ds(start, blk)]
        return _dot_tn(kb, qts[r0:r0 + HEAD_DIM])

    def past(n, carries):
        start = pl.multiple_of(n * blk, blk)
        dist = d0 + ((i - n) * blk).astype(F32)
        raw = [scores(h, start) for h in range(nh)]
        out = []
        for h in range(nh):
            st = raw[h] - slopes[h] * dist
            st = jnp.where(sel_ref[h, pl.ds(n, 1), :] > 0.5, st, NEG)
            out.append(head_step(h, carries[h], start, st))
        return tuple(out)

    init = tuple((jnp.full((1, blk), NEG, F32), jnp.zeros((1, blk), F32), jnp.zeros((HEAD_DIM, blk), F32))
                 for _ in range(nh))
    carries = lax.fori_loop(0, i, past, init)
    start = pl.multiple_of(i * blk, blk)
    outs = []
    raw = [scores(h, start) for h in range(nh)]
    for h in range(nh):
        st = raw[h] - slopes[h] * d0
        st = jnp.where(d0 >= 0.0, st, NEG)
        _, l_fin, acc = head_step(h, carries[h], start, st)
        outs.append(acc / l_fin)
    o_ref[...] = jnp.concatenate(outs, axis=0).T * g_ref[...]


def _moba_prompt(q, kt, vt, gate, slopes):
    b, t, _ = q.shape
    nb = t // MOBA_BLOCK
    width = HEADS_PER_STEP * HEAD_DIM
    qspec = pl.BlockSpec((None, MOBA_BLOCK, width), lambda bi, g, i: (bi, i, g))
    kspec = pl.BlockSpec((None, width, t), lambda bi, g, i: (bi, g, 0))
    return pl.pallas_call(
        functools.partial(_moba_prompt_kernel, nb),
        grid=(b, N_HEADS_A // HEADS_PER_STEP, nb),
        in_specs=[pl.BlockSpec(memory_space=pltpu.SMEM), qspec, kspec, kspec, qspec],
        out_specs=qspec,
        out_shape=jax.ShapeDtypeStruct(q.shape, F32),
        scratch_shapes=[pltpu.VMEM((width, LANES), F32), pltpu.VMEM((HEADS_PER_STEP, nb, MOBA_BLOCK), F32)],
        compiler_params=_params("arbitrary", "arbitrary", "arbitrary"),
        name="moba_prompt",
    )(slopes, q, kt, vt, gate)


SCORE_ROWS = 128


def _moba_sample_kernel(n_pages, page, pt_ref, q_ref, kn_ref, vn_ref, g_ref, rowc_ref, hm_ref, *rest):
    kp_refs = rest[:n_pages]
    vp_refs = rest[n_pages:2 * n_pages]
    o_ref = rest[2 * n_pages]
    s_ref = rest[2 * n_pages + 1]
    del pt_ref
    t_new = q_ref.shape[0]
    nh = N_HEADS_A
    rows = SCORE_ROWS
    pages_per_blk = MOBA_BLOCK // page
    nblk = n_pages // pages_per_blk
    hm = hm_ref[...]
    q = q_ref[...]
    qrep = jnp.concatenate([q] * (rows // t_new), axis=0) * hm
    qsb = _bf(qrep * QK_SCALE)
    qt = qrep.T
    slope = rowc_ref[:, 0:1]
    qpos = rowc_ref[:, 1:2]
    lane_f = lax.broadcasted_iota(jnp.int32, (rows, page), 1).astype(F32)

    ksum = [None] * nblk
    for j in range(n_pages):
        kp = kp_refs[j][...].reshape(nh * HEAD_DIM, page)
        n = j // pages_per_blk
        ksum[n] = kp if ksum[n] is None else ksum[n] + kp
        st = jnp.dot(qsb, _bf(kp), preferred_element_type=F32)
        dist = (qpos - float(j * page)) - lane_f
        s_ref[:, j * page:(j + 1) * page] = st - slope * dist

    grows = []
    for n in range(nblk):
        col = jnp.sum(ksum[n], axis=1, keepdims=True)
        grows.append(jnp.sum(qt * col, axis=0, keepdims=True))
    sel = _topk_rows(jnp.concatenate(grows, axis=0), nblk, nblk)
    sel_t = sel.T

    s_own = _dot_nt(qsb, kn_ref[...])
    trow = lax.broadcasted_iota(jnp.int32, (rows, t_new), 0) % t_new
    tcol = lax.broadcasted_iota(jnp.int32, (rows, t_new), 1)
    dist_own = (trow - tcol).astype(F32)
    s_own = jnp.where(dist_own >= 0.0, s_own - slope * dist_own, NEG)

    mvec = jnp.full((rows, page), NEG, F32)
    for j in range(n_pages):
        n = j // pages_per_blk
        st = jnp.where(sel_t[:, n:n + 1] > 0.5, s_ref[:, j * page:(j + 1) * page], NEG)
        s_ref[:, j * page:(j + 1) * page] = st
        mvec = jnp.maximum(mvec, st)
    m_row = jnp.maximum(jnp.max(mvec, axis=1, keepdims=True), jnp.max(s_own, axis=1, keepdims=True))

    p_own = jnp.exp(s_own - m_row)
    lvec = jnp.zeros((rows, page), F32)
    acc_t = jnp.zeros((nh * HEAD_DIM, rows), F32)
    for j in range(n_pages):
        pj = jnp.exp(s_ref[:, j * page:(j + 1) * page] - m_row)
        lvec = lvec + pj
        acc_t = acc_t + _dot_nt(vp_refs[j][...].reshape(nh * HEAD_DIM, page), pj)
    l_row = jnp.sum(lvec, axis=1, keepdims=True) + jnp.sum(p_own, axis=1, keepdims=True)
    tot = (acc_t.T + _dot(p_own, vn_ref[...])) * (1.0 / l_row) * hm
    out = tot[0:t_new]
    for h in range(1, nh):
        out = out + tot[h * t_new:(h + 1) * t_new]
    o_ref[...] = out * g_ref[...]


def _moba_sample(q, k_new, v_new, gate, cache_kt, cache_vt, pt_flat, layer, consts, n_seq, t_new, n_pages):
    page = cache_kt.shape[-1]
    rowc, hm = consts
    tspec = pl.BlockSpec((t_new, D_A), lambda b, pt: (b, 0))

    def page_spec(j):
        return pl.BlockSpec((None, None, N_HEADS_A, HEAD_DIM, page),
                            lambda b, pt, j=j: (pt[b * n_pages + j], layer, 0, 0, 0))

    cspec = lambda a: pl.BlockSpec(a.shape, lambda b, pt: (0,) * a.ndim)
    grid_spec = pltpu.PrefetchScalarGridSpec(
        num_scalar_prefetch=1,
        grid=(n_seq,),
        in_specs=[tspec, tspec, tspec, tspec, cspec(rowc), cspec(hm)]
        + [page_spec(j) for j in range(n_pages)] + [page_spec(j) for j in range(n_pages)],
        out_specs=tspec,
        scratch_shapes=[pltpu.VMEM((SCORE_ROWS, n_pages * page), F32)],
    )
    return pl.pallas_call(
        functools.partial(_moba_sample_kernel, n_pages, page),
        grid_spec=grid_spec,
        out_shape=jax.ShapeDtypeStruct(q.shape, F32),
        compiler_params=_params("arbitrary"),
        name="moba_sample",
    )(pt_flat, q, k_new, v_new, gate, rowc, hm, *([cache_kt] * n_pages), *([cache_vt] * n_pages))


def _moba_sample_consts(t_new, past_len):
    nh = N_HEADS_A
    used = nh * t_new
    r = np.arange(SCORE_ROWS)
    live = r < used
    slopes = 2.0 ** (-8.0 * (np.arange(nh) + 1) / nh)
    rowc = np.zeros((SCORE_ROWS, 2), np.float32)
    rowc[:, 0] = np.where(live, slopes[np.minimum(r // t_new, nh - 1)], 0.0)
    rowc[:, 1] = past_len + (r % t_new)
    c = np.arange(nh * HEAD_DIM)
    hm = (((c[None, :] // HEAD_DIM) == (r[:, None] // t_new)) & live[:, None]).astype(np.float32)
    return jnp.asarray(rowc), jnp.asarray(hm)


def _hgrn_consts(c):
    levels = int(round(math.log2(c)))
    t = np.arange(c)[:, None]
    u = np.arange(c)[None, :]
    mats = [u <= t, u > t]
    masks = [t == u]
    for j in range(1, levels + 1):
        p = 2 ** j
        hlf = p // 2
        mid = (t // p) * p + hlf
        upper = (t % p) >= hlf
        mats.append(upper & (u >= mid) & (u <= t))
        mats.append((~upper) & (u > t) & (u <= mid - 1))
        masks.append(((t // p) == (u // p)) & upper & ((u % p) < hlf))
    w_all = jnp.asarray(np.concatenate(mats, axis=0), dtype=BF16)
    mk = jnp.asarray(np.stack(masks), dtype=F32)
    return w_all, mk


def _hgrn_kernel(c, levels, q_ref, lf_ref, kk_ref, v_ref, g_ref, ng_ref, w_ref, mk_ref, e_ref,
                 o_ref, sn_ref, sbd_ref):
    ci = pl.program_id(1)
    nc = pl.num_programs(1)
    nh = N_HEADS_B
    width = nh * HEAD_DIM

    @pl.when(ci == 0)
    def _():
        sbd_ref[...] = jnp.zeros((width, width), F32)

    lf = lf_ref[...]
    e_all = jnp.exp(_dot_sel(w_ref[...], lf, parts=2))
    e_b = e_all[0:c]
    e_end = e_all[c:2 * c]
    q = q_ref[...]
    k = kk_ref[...]
    v = v_ref[...]
    vb = _bf(v)
    sbd = sbd_ref[...]
    o = _dot(q * e_b, sbd)
    lane = lax.broadcasted_iota(jnp.int32, (c, width), 1)
    qk_levels = [(q, k)]
    for j in range(1, levels + 1):
        qk_levels.append((q * e_all[2 * j * c:(2 * j + 1) * c], k * e_all[(2 * j + 1) * c:(2 * j + 2) * c]))
    heads = [(lane >= h * HEAD_DIM) & (lane < (h + 1) * HEAD_DIM) for h in range(nh)]
    kbs = [_bf(kj) for _, kj in qk_levels]
    a_heads = []
    for h in range(nh):
        a = jnp.zeros((c, c), F32)
        for j, (qj, _) in enumerate(qk_levels):
            a = a + _dot_nt(jnp.where(heads[h], qj, 0.0), kbs[j]) * mk_ref[j]
        a_heads.append(_bf(a))
    for h in range(nh):
        o = o + jnp.where(heads[h], jnp.dot(a_heads[h], vb, preferred_element_type=F32), 0.0)

    ones = jnp.ones((c, LANES), BF16)
    hi, mid, lo = _split3(lf)
    dn = (((0,), (0,)), ((), ()))
    colsum = (lax.dot_general(hi, ones, dn, preferred_element_type=F32)
              + lax.dot_general(mid, ones, dn, preferred_element_type=F32)
              + lax.dot_general(lo, ones, dn, preferred_element_type=F32))
    decay = jnp.exp(colsum)
    decay = jnp.concatenate([decay] * (width // LANES), axis=1)
    r = lax.broadcasted_iota(jnp.int32, (width, width), 0) // HEAD_DIM
    cc = lax.broadcasted_iota(jnp.int32, (width, width), 1) // HEAD_DIM
    s_new = sbd * decay + jnp.where(r == cc, _dot_tn(k * e_end, v), 0.0)
    sbd_ref[...] = s_new

    ss = _seg_sum(o * o, e_ref)
    o_ref[...] = o * lax.rsqrt(ss * (1.0 / HEAD_DIM) + EPS) * ng_ref[...] * g_ref[...]

    @pl.when(ci == nc - 1)
    def _():
        for h in range(nh):
            sn_ref[h] = s_new[h * HEAD_DIM:(h + 1) * HEAD_DIM, h * HEAD_DIM:(h + 1) * HEAD_DIM]


def _hgrn(q, lf, kk, v, gate, ng, consts, e256, c):
    b, t, width = q.shape
    w_all, mk = consts
    levels = mk.shape[0] - 1
    tok = pl.BlockSpec((None, c, width), lambda bi, ci: (bi, ci, 0))
    st = pl.BlockSpec((None, N_HEADS_B, HEAD_DIM, HEAD_DIM), lambda bi, ci: (bi, 0, 0, 0))
    return pl.pallas_call(
        functools.partial(_hgrn_kernel, c, levels),
        grid=(b, t // c),
        in_specs=[tok, tok, tok, tok, tok, _const_spec((1, width)), _const_spec(w_all.shape),
                  _const_spec(mk.shape), _const_spec((256, 256))],
        out_specs=[tok, st],
        out_shape=[jax.ShapeDtypeStruct(q.shape, F32),
                   jax.ShapeDtypeStruct((b, N_HEADS_B, HEAD_DIM, HEAD_DIM), F32)],
        scratch_shapes=[pltpu.VMEM((width, width), F32)],
        compiler_params=_params("arbitrary", "arbitrary"),
        name="hgrn",
    )(q, lf, kk, v, gate, ng, w_all, mk, e256)


def _mlstm_consts(c):
    t = np.arange(c)[:, None]
    u = np.arange(c)[None, :]
    return jnp.asarray(u <= t, dtype=BF16)


def _mlstm_kernel(c, uc_ref, g_ref, cw_ref, cb_ref, wq_ref, wk_ref, wv_ref,
                  wi_ref, wf_ref, bi_ref, bf_ref, skip_ref, ng_ref, tri_ref, e_ref,
                  o_ref, cn_ref, nn_ref, mn_ref, cvn_ref,
                  uext_ref, cbd_ref, n_ref, m_ref):
    ci = pl.program_id(1)
    nc = pl.num_programs(1)
    nh = N_HEADS_C
    width = nh * HEAD_DIM

    @pl.when(ci == 0)
    def _():
        uext_ref[0:8, :] = jnp.zeros((8, width), F32)
        cbd_ref[...] = jnp.zeros((width, width), F32)
        n_ref[...] = jnp.zeros((1, width), F32)
        m_ref[...] = jnp.zeros((1, GATE_LANES), F32)

    uc = uc_ref[...]
    uext_ref[8:8 + c, :] = uc
    conv = cb_ref[...]
    for j in range(CONV_W):
        conv = conv + cw_ref[j:j + 1, :] * uext_ref[5 + j:5 + j + c, :]
    tail = uext_ref[c:c + 8, :]
    uext_ref[0:8, :] = tail
    uconv = _silu(conv)

    qm = _dot(uconv, wq_ref[...])
    km = _dot(uconv, wk_ref[...])
    vm = _dot(uc, wv_ref[...])
    i_raw = _dot3(qm, wi_ref[0]) + _dot3(km, wi_ref[1]) + _dot3(vm, wi_ref[2]) + bi_ref[...]
    f_pre = _dot3(qm, wf_ref[0]) + _dot3(km, wf_ref[1]) + _dot3(vm, wf_ref[2]) + bf_ref[...]
    logf = _log_sigmoid(f_pre)
    bcum = _dot_sel(tri_ref[...], logf)
    a = i_raw - bcum
    rowi = lax.broadcasted_iota(jnp.int32, (c, GATE_LANES), 0)
    s = 1
    while s < c:
        a = jnp.maximum(a, jnp.where(rowi >= s, pltpu.roll(a, s, 0), -jnp.inf))
        s *= 2
    m0 = m_ref[...]
    m_t = bcum + jnp.maximum(m0, a)
    g_in = jnp.exp(bcum + m0 - m_t)
    bm = bcum - m_t
    ib_t = (i_raw - bcum).T
    m_end = m_t[c - 1:c, :]
    b_end = bcum[c - 1:c, :]
    w_tok = jnp.exp((b_end - bcum) + i_raw - m_end)
    g_end = jnp.exp(b_end + m0 - m_end)

    ks = km * QK_SCALE
    ksb = _bf(ks)
    vmb = _bf(vm)
    lane = lax.broadcasted_iota(jnp.int32, (c, width), 1)
    trow = lax.broadcasted_iota(jnp.int32, (c, c), 0)
    tcol = lax.broadcasted_iota(jnp.int32, (c, c), 1)
    causal = tcol <= trow
    g256 = _spread_heads(g_in, c, nh)
    cbd = cbd_ref[...]
    n0 = n_ref[...]
    num = g256 * _dot(qm, cbd)
    qn = qm * n0
    den_cols = jnp.zeros((c, GATE_LANES), F32)
    lane_g = lax.broadcasted_iota(jnp.int32, (c, GATE_LANES), 1)
    heads = [(lane >= h * HEAD_DIM) & (lane < (h + 1) * HEAD_DIM) for h in range(nh)]
    raw = [_dot_nt(jnp.where(heads[h], qm, 0.0), ksb) for h in range(nh)]
    for h in range(nh):
        expo = bm[:, h:h + 1] + ib_t[h:h + 1, :]
        dmat = jnp.exp(jnp.where(causal, expo, NEG))
        qk = raw[h] * dmat
        num = num + jnp.where(heads[h], jnp.dot(_bf(qk), vmb, preferred_element_type=F32), 0.0)
        den_h = (g_in[:, h:h + 1] * jnp.sum(jnp.where(heads[h], qn, 0.0), axis=1, keepdims=True)
                 + jnp.sum(qk, axis=1, keepdims=True))
        den_cols = jnp.where(lane_g == h, den_h, den_cols)
    denom = jnp.maximum(jnp.abs(den_cols), jnp.exp(-m_t))
    hval = num / _spread_heads(denom, c, nh)

    w256 = _spread_heads(w_tok, c, nh)
    gend256 = _spread_heads(g_end, 1, nh)
    r = lax.broadcasted_iota(jnp.int32, (width, width), 0) // HEAD_DIM
    cc = lax.broadcasted_iota(jnp.int32, (width, width), 1) // HEAD_DIM
    kw = ks * w256
    c_new = cbd * gend256 + jnp.where(r == cc, _dot_tn(kw, vm), 0.0)
    n_new = gend256 * n0 + jnp.sum(kw, axis=0, keepdims=True)
    cbd_ref[...] = c_new
    n_ref[...] = n_new
    m_ref[...] = m_end

    mean = _seg_sum(hval, e_ref) * (1.0 / HEAD_DIM)
    xc = hval - mean
    var = _seg_sum(xc * xc, e_ref) * (1.0 / HEAD_DIM)
    hc = xc * lax.rsqrt(var + EPS) * ng_ref[...]
    o_ref[...] = (hc + skip_ref[...] * uconv) * g_ref[...]

    @pl.when(ci == nc - 1)
    def _():
        for h in range(nh):
            cn_ref[h] = c_new[h * HEAD_DIM:(h + 1) * HEAD_DIM, h * HEAD_DIM:(h + 1) * HEAD_DIM]
        nn_ref[...] = n_new
        mn_ref[...] = m_end
        cvn_ref[...] = tail


def _mlstm(uc, gate, lw, tri, e256, c):
    b, t, width = uc.shape
    tok = pl.BlockSpec((None, c, width), lambda bi, ci: (bi, ci, 0))
    per_b = lambda shp: pl.BlockSpec((None,) + shp, lambda bi, ci: (bi,) + (0,) * len(shp))
    weights = [lw['conv_w'], lw['conv_b'], lw['wq'], lw['wk'], lw['wv'], lw['wi'], lw['wf'], lw['bi'], lw['bf'],
               lw['skip'], lw['mng'], tri, e256]
    out_shape = [jax.ShapeDtypeStruct(uc.shape, F32),
                 jax.ShapeDtypeStruct((b, N_HEADS_C, HEAD_DIM, HEAD_DIM), F32),
                 jax.ShapeDtypeStruct((b, 1, width), F32), jax.ShapeDtypeStruct((b, 1, GATE_LANES), F32),
                 jax.ShapeDtypeStruct((b, 8, width), F32)]
    return pl.pallas_call(
        functools.partial(_mlstm_kernel, c),
        grid=(b, t // c),
        in_specs=[tok, tok] + [_const_spec(w.shape) for w in weights],
        out_specs=[tok, per_b((N_HEADS_C, HEAD_DIM, HEAD_DIM)), per_b((1, width)), per_b((1, GATE_LANES)),
                   per_b((8, width))],
        out_shape=out_shape,
        scratch_shapes=[pltpu.VMEM((c + 8, width), F32), pltpu.VMEM((width, width), F32),
                        pltpu.VMEM((1, width), F32), pltpu.VMEM((1, GATE_LANES), F32)],
        compiler_params=_params("arbitrary", "arbitrary"),
        name="mlstm",
    )(uc, gate, *weights)


K_UNROLL = 8


def _hgrn_sample_kernel(t_new, q_ref, kk_ref, v_ref, g_ref, s0_ref, ng_ref, o_ref, sn_ref):
    sn_ref[...] = s0_ref[...]
    for t in range(t_new):
        vt = v_ref[t]

        def kbody(kb, o, t=t):
            for kk in range(K_UNROLL):
                k = kb * K_UNROLL + kk
                kt = kk_ref[t, pl.ds(k, 1), :]
                s_k = (1.0 - kt) * sn_ref[k] + kt * vt
                sn_ref[k] = s_k
                o = o + s_k * q_ref[t, pl.ds(k, 1), :]
            return o

        o = lax.fori_loop(0, HEAD_DIM // K_UNROLL, kbody, jnp.zeros(vt.shape, F32))
        ss = jnp.sum(o * o, axis=0, keepdims=True) * (1.0 / HEAD_DIM)
        o_ref[t] = o * lax.rsqrt(ss + EPS) * ng_ref[...] * g_ref[t]


def _hgrn_sample(qt, kkt, vt, gt, s0, ng_col):
    t_new, width, b = qt.shape
    tok = pl.BlockSpec((t_new, HEAD_DIM, b), lambda h: (0, h, 0))
    st = pl.BlockSpec((None, HEAD_DIM, HEAD_DIM, b), lambda h: (h, 0, 0, 0))
    return pl.pallas_call(
        functools.partial(_hgrn_sample_kernel, t_new),
        grid=(N_HEADS_B,),
        in_specs=[tok, tok, tok, tok, st, pl.BlockSpec((HEAD_DIM, 1), lambda h: (h, 0))],
        out_specs=[tok, st],
        out_shape=[jax.ShapeDtypeStruct(qt.shape, F32), jax.ShapeDtypeStruct(s0.shape, F32)],
        compiler_params=_params("arbitrary"),
        name="hgrn_sample",
    )(qt, kkt, vt, gt, s0, ng_col)


def _mlstm_front_kernel(t_new, uc_ref, cv0_ref, cw_ref, cb_ref, wq_ref, wk_ref, wv_ref, wi_ref, wf_ref,
                        bi_ref, bf_ref, uconv_ref, qm_ref, km_ref, vm_ref, i_ref, lf_ref):
    hist = [cv0_ref[j] for j in range(CONV_W - 1)] + [uc_ref[t] for t in range(t_new)]
    for t in range(t_new):
        conv = cb_ref[...]
        for j in range(CONV_W):
            conv = conv + cw_ref[j] * hist[t + j]
        uconv = _silu(conv)
        uconv_ref[t] = uconv
        qm = _dot(wq_ref[...], uconv)
        km = _dot(wk_ref[...], uconv)
        vm = _dot(wv_ref[...], hist[t + CONV_W - 1])
        qm_ref[t] = qm
        km_ref[t] = km
        vm_ref[t] = vm
        i_ref[t] = _dot3(wi_ref[0], qm) + _dot3(wi_ref[1], km) + _dot3(wi_ref[2], vm) + bi_ref[...]
        lf_ref[t] = _log_sigmoid(_dot3(wf_ref[0], qm) + _dot3(wf_ref[1], km) + _dot3(wf_ref[2], vm)
                                 + bf_ref[...])


def _mlstm_front(uct, cv0t, lw):
    t_new, width, b = uct.shape
    weights = [lw['conv_w_col'], lw['conv_b_col'], lw['wq_t'], lw['wk_t'], lw['wv_t'], lw['wi_t'], lw['wf_t'],
               lw['bi_col'], lw['bf_col']]
    big = jax.ShapeDtypeStruct(uct.shape, F32)
    small = jax.ShapeDtypeStruct((t_new, SUBLANES, b), F32)
    return pl.pallas_call(
        functools.partial(_mlstm_front_kernel, t_new),
        grid=(1,),
        in_specs=[_const_spec(uct.shape), _const_spec(cv0t.shape)] + [_const_spec(w.shape) for w in weights],
        out_specs=[_const_spec(uct.shape)] * 4 + [_const_spec(small.shape)] * 2,
        out_shape=[big] * 4 + [small] * 2,
        compiler_params=_params("arbitrary"),
        name="mlstm_front",
    )(uct, cv0t, *weights)


def _mlstm_sample_kernel(t_new, qm_ref, km_ref, vm_ref, i_ref, lf_ref, uconv_ref, g_ref, c0_ref, n0_ref, m0_ref,
                         skip_ref, ng_ref, o_ref, cn_ref, nn_ref, mn_ref):
    h = pl.program_id(0)
    cn_ref[...] = c0_ref[...]
    n = n0_ref[...]
    m = m0_ref[...]
    for t in range(t_new):
        i_t = i_ref[t, pl.ds(h, 1), :]
        lf_t = lf_ref[t, pl.ds(h, 1), :]
        m_new = jnp.maximum(lf_t + m, i_t)
        fp = jnp.exp(lf_t + m - m_new)
        ip = jnp.exp(i_t - m_new)
        vt = vm_ref[t]
        n = fp * n + ip * (km_ref[t] * QK_SCALE)

        def kbody(kb, num, t=t, fp=fp, ip=ip, vt=vt):
            for kk in range(K_UNROLL):
                k = kb * K_UNROLL + kk
                kt = km_ref[t, pl.ds(k, 1), :] * QK_SCALE
                c_k = fp * cn_ref[k] + (ip * kt) * vt
                cn_ref[k] = c_k
                num = num + c_k * qm_ref[t, pl.ds(k, 1), :]
            return num

        num = lax.fori_loop(0, HEAD_DIM // K_UNROLL, kbody, jnp.zeros(vt.shape, F32))
        den = jnp.sum(qm_ref[t] * n, axis=0, keepdims=True)
        hval = num / jnp.maximum(jnp.abs(den), jnp.exp(-m_new))
        m = m_new
        mean = jnp.sum(hval, axis=0, keepdims=True) * (1.0 / HEAD_DIM)
        xc = hval - mean
        var = jnp.sum(xc * xc, axis=0, keepdims=True) * (1.0 / HEAD_DIM)
        hc = xc * lax.rsqrt(var + EPS) * ng_ref[...]
        o_ref[t] = (hc + skip_ref[...] * uconv_ref[t]) * g_ref[t]
    nn_ref[...] = n
    mn_ref[...] = m


def _mlstm_sample(front, gt, c0, n0, m0, lw):
    uconv, qm, km, vm, i_raw, logf = front
    t_new, width, b = qm.shape
    tok = pl.BlockSpec((t_new, HEAD_DIM, b), lambda h: (0, h, 0))
    gates = _const_spec(i_raw.shape)
    st = pl.BlockSpec((None, HEAD_DIM, HEAD_DIM, b), lambda h: (h, 0, 0, 0))
    nst = pl.BlockSpec((None, HEAD_DIM, b), lambda h: (h, 0, 0))
    mst = pl.BlockSpec((None, 1, b), lambda h: (h, 0, 0))
    col = pl.BlockSpec((HEAD_DIM, 1), lambda h: (h, 0))
    return pl.pallas_call(
        functools.partial(_mlstm_sample_kernel, t_new),
        grid=(N_HEADS_C,),
        in_specs=[tok, tok, tok, gates, gates, tok, tok, st, nst, mst, col, col],
        out_specs=[tok, st, nst, mst],
        out_shape=[jax.ShapeDtypeStruct(qm.shape, F32), jax.ShapeDtypeStruct(c0.shape, F32),
                   jax.ShapeDtypeStruct(n0.shape, F32), jax.ShapeDtypeStruct(m0.shape, F32)],
        compiler_params=_params("arbitrary"),
        name="mlstm_sample",
    )(qm, km, vm, i_raw, logf, uconv, gt, c0, n0, m0, lw['skip_col'], lw['mng_col'])


def _gate_weights(w):
    w3 = w.reshape(3, D_C, N_HEADS_C)
    return jnp.pad(w3, ((0, 0), (0, 0), (0, GATE_LANES - N_HEADS_C)))


def _gate_weights_t(w):
    w3 = jnp.swapaxes(w.reshape(3, D_C, N_HEADS_C), 1, 2)
    return jnp.pad(w3, ((0, 0), (0, SUBLANES - N_HEADS_C), (0, 0)))


def _pad_lanes(v):
    return jnp.pad(v, ((0, 0),) * (v.ndim - 1) + ((0, GATE_LANES - v.shape[-1]),))


def _pad_rows_col(v):
    return jnp.pad(v, (0, SUBLANES - v.shape[0]))[:, None]


def _layer_weights(l, norm_g, w_in, q_norm_g, k_norm_g, hgrn_norm_g, mlstm_conv_w, mlstm_conv_b, mlstm_wq,
                   mlstm_wk, mlstm_wv, mlstm_w_ig, mlstm_b_ig, mlstm_w_fg, mlstm_b_fg, mlstm_skip, mlstm_norm_g,
                   w_out):
    bd = lambda w: _block_diag(w).astype(BF16)
    bdt = lambda w: _block_diag(jnp.swapaxes(w, 1, 2)).astype(BF16)
    return {
        'norm_g': norm_g[l][None, :],
        'w_in': w_in[l].astype(BF16),
        'qg': jnp.tile(q_norm_g[l], N_HEADS_A)[None, :],
        'kg': jnp.tile(k_norm_g[l], N_HEADS_A)[None, :],
        'hng': hgrn_norm_g[l][None, :],
        'hng_col': hgrn_norm_g[l][:, None],
        'conv_w': mlstm_conv_w[l],
        'conv_b': mlstm_conv_b[l][None, :],
        'conv_w_col': mlstm_conv_w[l][:, :, None],
        'conv_b_col': mlstm_conv_b[l][:, None],
        'wq': bd(mlstm_wq[l]), 'wk': bd(mlstm_wk[l]), 'wv': bd(mlstm_wv[l]),
        'wq_t': bdt(mlstm_wq[l]), 'wk_t': bdt(mlstm_wk[l]), 'wv_t': bdt(mlstm_wv[l]),
        'wi': _gate_weights(mlstm_w_ig[l]), 'wf': _gate_weights(mlstm_w_fg[l]),
        'wi_t': _gate_weights_t(mlstm_w_ig[l]), 'wf_t': _gate_weights_t(mlstm_w_fg[l]),
        'bi': _pad_lanes(mlstm_b_ig[l][None, :]), 'bf': _pad_lanes(mlstm_b_fg[l][None, :]),
        'bi_col': _pad_rows_col(mlstm_b_ig[l]), 'bf_col': _pad_rows_col(mlstm_b_fg[l]),
        'skip': mlstm_skip[l][None, :], 'mng': mlstm_norm_g[l][None, :],
        'skip_col': mlstm_skip[l][:, None], 'mng_col': mlstm_norm_g[l][:, None],
        'wo_a': w_out[l][:D_A].astype(BF16),
        'wo_b': w_out[l][D_A:D_A + D_B].astype(BF16),
        'wo_c': w_out[l][D_A + D_B:].astype(BF16),
    }


def _prompt_layer(x, layer, lw, shared, chunk, tm):
    b, t, d = x.shape
    x2d = x.reshape(b * t, d)
    (qa, kt, vt, ga, qb, lf, kk, ib, gb, uc, gc) = _inproj(x2d, layer, lw, shared, tm, seq_len=t)
    r3 = lambda a: a.reshape(b, t, a.shape[-1])
    oa = _moba_prompt(r3(qa), kt, vt, r3(ga), shared['slopes'])
    ob, s_new = _hgrn(r3(qb), r3(lf), r3(kk), r3(ib), r3(gb), lw['hng'], shared['hgrn_consts'], shared['e256'],
                      chunk)
    oc, c_new, n_new, m_new, cv_new = _mlstm(r3(uc), r3(gc), lw, shared['mlstm_tri'], shared['e256'], chunk)
    y = _outproj(x2d, oa.reshape(b * t, D_A), ob.reshape(b * t, D_B), oc.reshape(b * t, D_C), lw, tm)
    return (y.reshape(b, t, d), kt, vt, s_new, c_new, n_new.reshape(b, N_HEADS_C, HEAD_DIM),
            m_new[:, 0, :N_HEADS_C], cv_new[:, SUBLANES - (CONV_W - 1):, :])


def _sample_layer(x, layer, lw, shared, cache_kt, cache_vt, pt_flat, s0, c0, n0, m0, cv0t, tm):
    b, t, d = x.shape
    n_pages = pt_flat.shape[0] // b
    x2d = x.reshape(b * t, d)
    (qa, ka, va, ga, qb, lf, kk, ib, gb, uc, gc) = _inproj(x2d, layer, lw, shared, tm)
    del lf
    oa = _moba_sample(qa, ka, va, ga, cache_kt, cache_vt, pt_flat, layer, shared['sample_consts'], b, t, n_pages)
    to_lanes = lambda a: jnp.transpose(a.reshape(b, t, a.shape[-1]), (1, 2, 0))
    from_lanes = lambda a: jnp.transpose(a, (2, 0, 1)).reshape(b * t, a.shape[1])
    obt, s_new = _hgrn_sample(to_lanes(qb), to_lanes(kk), to_lanes(ib), to_lanes(gb), s0, lw['hng_col'])
    front = _mlstm_front(to_lanes(uc), cv0t, lw)
    oct, c_new, n_new, m_new = _mlstm_sample(front, to_lanes(gc), c0, n0, m0, lw)
    y = _outproj(x2d, oa, from_lanes(obt), from_lanes(oct), lw, tm)
    conv_new = uc.reshape(b, t, D_C)[:, t - (CONV_W - 1):, :]
    return (y.reshape(b, t, d), ka.reshape(b, t, N_HEADS_A, HEAD_DIM), va.reshape(b, t, N_HEADS_A, HEAD_DIM),
            s_new, c_new, n_new, m_new, conv_new)


def kernel(x_prompt, x_sample, cache_k, cache_v, page_table, state_hgrn, state_mlstm_c, state_mlstm_n,
           state_mlstm_m, state_mlstm_conv, norm_g, w_in, q_norm_g, k_norm_g, hgrn_lb, hgrn_norm_g,
           mlstm_conv_w, mlstm_conv_b, mlstm_wq, mlstm_wk, mlstm_wv, mlstm_w_ig, mlstm_b_ig, mlstm_w_fg,
           mlstm_b_fg, mlstm_skip, mlstm_norm_g, w_out):
    depth = w_in.shape[0]
    bp, tp, _ = x_prompt.shape
    bd, td, _ = x_sample.shape
    n_pages = page_table.shape[1]
    page = cache_k.shape[2]
    chunk = min(tp, 256)
    shared = {
        'e256': _head_block_ones(),
        'hgrn_lb': hgrn_lb.astype(F32),
        'hgrn_consts': _hgrn_consts(chunk),
        'mlstm_tri': _mlstm_consts(chunk),
        'slopes': jnp.asarray(2.0 ** (-8.0 * (np.arange(N_HEADS_A) + 1) / N_HEADS_A), dtype=F32),
        'sample_consts': _moba_sample_consts(td, n_pages * page),
    }
    pt_flat = page_table.reshape(-1).astype(jnp.int32)
    cache_kt = jnp.transpose(cache_k, (0, 1, 3, 4, 2))
    cache_vt = jnp.transpose(cache_v, (0, 1, 3, 4, 2))
    s_h = jnp.transpose(state_hgrn, (0, 2, 3, 4, 1))
    s_c = jnp.transpose(state_mlstm_c, (0, 2, 3, 4, 1))
    s_n = jnp.transpose(state_mlstm_n, (0, 2, 3, 1))
    s_m = jnp.transpose(state_mlstm_m, (0, 2, 1))[:, :, None, :]
    s_cv = jnp.transpose(state_mlstm_conv, (0, 2, 3, 1))

    yp, ys = x_prompt, x_sample
    outs_p = [[] for _ in range(7)]
    outs_s = [[] for _ in range(7)]
    for l in range(depth):
        lw = _layer_weights(l, norm_g, w_in, q_norm_g, k_norm_g, hgrn_norm_g, mlstm_conv_w, mlstm_conv_b,
                            mlstm_wq, mlstm_wk, mlstm_wv, mlstm_w_ig, mlstm_b_ig, mlstm_w_fg, mlstm_b_fg,
                            mlstm_skip, mlstm_norm_g, w_out)
        res_p = _prompt_layer(yp, l, lw, shared, chunk, 256)
        res_s = _sample_layer(ys, l, lw, shared, cache_kt, cache_vt, pt_flat, s_h[l], s_c[l], s_n[l], s_m[l],
                              s_cv[l], 256)
        yp, ys = res_p[0], res_s[0]
        for acc, a in zip(outs_p, res_p[1:]):
            acc.append(a)
        for acc, a in zip(outs_s, res_s[1:]):
            acc.append(a)

    st = lambda lst, ax: jnp.stack(lst, axis=ax)

    def kv_prompt(lst):
        a = st(lst, 1).reshape(bp, depth, N_HEADS_A, HEAD_DIM, tp)
        return jnp.transpose(a, (0, 1, 4, 2, 3))

    batch_first = lambda a: jnp.moveaxis(a, -1, 1)
    return (yp, ys, kv_prompt(outs_p[0]), kv_prompt(outs_p[1]), st(outs_s[0], 1), st(outs_s[1], 1),
            st(outs_p[2], 0), batch_first(st(outs_s[2], 0)), st(outs_p[3], 0), batch_first(st(outs_s[3], 0)),
            st(outs_p[4], 0), batch_first(st(outs_s[4], 0)), st(outs_p[5], 0),
            batch_first(st(outs_s[5], 0)[:, :, 0, :]), st(outs_p[6], 0), st(outs_s[6], 0))
```

```python
import functools
import math

import numpy as np
import jax
import jax.numpy as jnp
from jax import lax
from jax.experimental import pallas as pl
from jax.experimental.pallas import tpu as pltpu

F32 = jnp.float32
BF16 = jnp.bfloat16

HEAD_DIM = 64
N_HEADS_A = 8
N_HEADS_B = 4
N_HEADS_C = 4
D_A = N_HEADS_A * HEAD_DIM
D_B = N_HEADS_B * HEAD_DIM
D_C = N_HEADS_C * HEAD_DIM
MOBA_BLOCK = 256
MOBA_TOPK = 3
CONV_W = 4
EPS = 1e-6
NEG = -1e30
GATE_LANES = 128
LANES = 128
SUBLANES = 8
VMEM_LIMIT = 56 * 1024 * 1024
PROMPT_ROW_TILE = 512
SAMPLE_ROW_TILE = 256
QK_SCALE = HEAD_DIM ** -0.5
LOG2E = math.log2(math.e)


def _bf(x):
    return x.astype(BF16)


def _dot(a, b):
    return jnp.dot(_bf(a), _bf(b), preferred_element_type=F32)


def _dot_nt(a, b):
    return lax.dot_general(_bf(a), _bf(b), (((1,), (1,)), ((), ())), preferred_element_type=F32)


def _dot_tn(a, b):
    return lax.dot_general(_bf(a), _bf(b), (((0,), (0,)), ((), ())), preferred_element_type=F32)


def _split2(x):
    hi = _bf(x)
    lo = _bf(x - hi.astype(F32))
    return hi, lo


def _split3(x):
    hi = _bf(x)
    r = x - hi.astype(F32)
    mid = _bf(r)
    lo = _bf(r - mid.astype(F32))
    return hi, mid, lo


def _dot_sel(w01, x, parts=3):
    pieces = _split3(x)[:parts]
    out = jnp.dot(w01, pieces[0], preferred_element_type=F32)
    for p in pieces[1:]:
        out = out + jnp.dot(w01, p, preferred_element_type=F32)
    return out


def _dot3(a, b):
    ah, al = _split2(a)
    bh, bl = _split2(b)
    return (jnp.dot(ah, bh, preferred_element_type=F32) + jnp.dot(al, bh, preferred_element_type=F32)
            + jnp.dot(ah, bl, preferred_element_type=F32))


def _dot3_tn(a, b):
    ah, al = _split2(a)
    bh, bl = _split2(b)
    dn = (((0,), (0,)), ((), ()))
    return (lax.dot_general(ah, bh, dn, preferred_element_type=F32)
            + lax.dot_general(al, bh, dn, preferred_element_type=F32)
            + lax.dot_general(ah, bl, dn, preferred_element_type=F32))


def _seg_sum(x, e_ref):
    hi, lo = _split2(x)
    e = e_ref[...]
    return jnp.dot(hi, e, preferred_element_type=F32) + jnp.dot(lo, e, preferred_element_type=F32)


def _spread_heads(x, rows, n_heads):
    lane = lax.broadcasted_iota(jnp.int32, (rows, n_heads * HEAD_DIM), 1)
    out = jnp.zeros((rows, n_heads * HEAD_DIM), F32)
    for h in range(n_heads):
        in_head = (lane >= h * HEAD_DIM) & (lane < (h + 1) * HEAD_DIM)
        out = jnp.where(in_head, x[:, h:h + 1], out)
    return out


def _silu(x):
    return x * jax.nn.sigmoid(x)


def _log_sigmoid(x):
    return jnp.minimum(x, 0.0) - jnp.log(1.0 + jnp.exp(-jnp.abs(x)))


def _topk_rows(g, n_rows, limit):
    rid = lax.broadcasted_iota(jnp.int32, g.shape, 0)
    g = jnp.where(rid < limit, g, -jnp.inf)
    cnt = jnp.zeros(g.shape, jnp.int32)
    for m in range(n_rows):
        row = g[m:m + 1, :]
        beats = (row > g) | ((row == g) & (m < rid))
        cnt = cnt + beats.astype(jnp.int32)
    return ((cnt < MOBA_TOPK) & (rid < limit)).astype(F32)


def _const_spec(shape):
    nd = len(shape)
    return pl.BlockSpec(shape, lambda *_: (0,) * nd)


def _head_block_ones():
    r = np.arange(256)
    return jnp.asarray((r[:, None] // HEAD_DIM) == (r[None, :] // HEAD_DIM), dtype=BF16)


def _block_diag(w):
    h = w.shape[0]
    eye = jnp.eye(h, dtype=w.dtype)
    return jnp.einsum('hde,hg->hdge', w, eye).reshape(h * HEAD_DIM, h * HEAD_DIM)


def _params(*sem):
    return pltpu.CompilerParams(dimension_semantics=sem, vmem_limit_bytes=VMEM_LIMIT)


def _inproj_kernel(layer, kv_transposed, x_ref, g_ref, w_ref, qg_ref, kg_ref, lb_ref, e_ref,
                   qa_ref, ka_ref, va_ref, ga_ref, qb_ref, lf_ref, kk_ref, ib_ref, gb_ref, uc_ref, gc_ref):
    x = x_ref[...]
    h = x * lax.rsqrt(jnp.mean(x * x, axis=-1, keepdims=True) + EPS) * g_ref[...]
    hb = _bf(h)

    def proj(c0, width):
        return jnp.dot(hb, w_ref[:, c0:c0 + width], preferred_element_type=F32)

    def head_rms(p, g):
        halves = [_seg_sum(p[:, c:c + 256] * p[:, c:c + 256], e_ref) for c in (0, 256)]
        ss = jnp.concatenate(halves, axis=1)
        return p * lax.rsqrt(ss * (1.0 / HEAD_DIM) + EPS) * g

    qa_ref[...] = head_rms(proj(0, D_A), qg_ref[...])
    ka = head_rms(proj(D_A, D_A), kg_ref[...])
    va = proj(2 * D_A, D_A)
    if kv_transposed:
        ka_ref[...] = ka.T
        va_ref[...] = va.T
    else:
        ka_ref[...] = ka
        va_ref[...] = va
    ga_ref[...] = _silu(proj(3 * D_A, D_A))
    c = 4 * D_A
    qb_ref[...] = proj(c, D_B)
    lbp = lb_ref[...]
    lbe = jnp.exp(lbp - jnp.max(lbp, axis=0, keepdims=True))
    lbw = lbe / jnp.sum(lbe, axis=0, keepdims=True)
    lb_cum = lbw[0:1, :]
    for j in range(1, layer + 1):
        lb_cum = lb_cum + lbw[j:j + 1, :]
    lb = lb_cum - lbw[0:1, :]
    fg = lb + (1.0 - lb) * jax.nn.sigmoid(proj(c + D_B, D_B))
    lf_ref[...] = jnp.log(fg)
    kk_ref[...] = 1.0 - fg
    ib_ref[...] = proj(c + 2 * D_B, D_B)
    gb_ref[...] = _silu(proj(c + 3 * D_B, D_B))
    c = c + 4 * D_B
    uc_ref[...] = proj(c, D_C)
    gc_ref[...] = _silu(proj(c + D_C, D_C))


def _inproj(x2d, layer, lw, shared, tm, seq_len=None):
    m, d = x2d.shape
    kv_transposed = seq_len is not None
    widths = [D_A] * 4 + [D_B] * 5 + [D_C] * 2
    out_shape = [jax.ShapeDtypeStruct((m, w), F32) for w in widths]
    out_specs = [pl.BlockSpec((tm, w), lambda i: (i, 0)) for w in widths]
    if kv_transposed:
        tiles = seq_len // tm
        for idx in (1, 2):
            out_shape[idx] = jax.ShapeDtypeStruct((m // seq_len, D_A, seq_len), F32)
            out_specs[idx] = pl.BlockSpec((None, D_A, tm), lambda i: (i // tiles, 0, i % tiles))
    w_bf = lw['w_in']
    return pl.pallas_call(
        functools.partial(_inproj_kernel, layer, kv_transposed),
        grid=(m // tm,),
        in_specs=[pl.BlockSpec((tm, d), lambda i: (i, 0)),
                  _const_spec((1, d)), _const_spec(w_bf.shape), _const_spec((1, D_A)), _const_spec((1, D_A)),
                  _const_spec(shared['hgrn_lb'].shape), _const_spec((256, 256))],
        out_specs=out_specs,
        out_shape=out_shape,
        compiler_params=_params("arbitrary"),
        name="inproj",
    )(x2d, lw['norm_g'], w_bf, lw['qg'], lw['kg'], shared['hgrn_lb'], shared['e256'])


def _outproj_kernel(x_ref, oa_ref, ob_ref, oc_ref, wa_ref, wb_ref, wc_ref, y_ref):
    y_ref[...] = (x_ref[...] + _dot(oa_ref[...], wa_ref[...]) + _dot(ob_ref[...], wb_ref[...])
                  + _dot(oc_ref[...], wc_ref[...]))


def _outproj(x2d, oa, ob, oc, lw, tm):
    m, d = x2d.shape
    row = lambda w: pl.BlockSpec((tm, w), lambda i: (i, 0))
    wa, wb, wc = lw['wo_a'], lw['wo_b'], lw['wo_c']
    return pl.pallas_call(
        _outproj_kernel,
        grid=(m // tm,),
        in_specs=[row(d), row(D_A), row(D_B), row(D_C), _const_spec(wa.shape), _const_spec(wb.shape),
                  _const_spec(wc.shape)],
        out_specs=row(d),
        out_shape=jax.ShapeDtypeStruct((m, d), F32),
        compiler_params=_params("arbitrary"),
        name="outproj",
    )(x2d, oa, ob, oc, wa, wb, wc)


HEADS_PER_STEP = 4


def _moba_prompt_kernel(nb, slopes_ref, q_ref, kt_ref, vt_ref, g_ref, o_ref, kmt_ref, sel_ref, sd_ref, sdo_ref,
                        raw_ref):
    g = pl.program_id(1)
    i = pl.program_id(2)
    blk = MOBA_BLOCK
    nh = HEADS_PER_STEP
    width = nh * HEAD_DIM

    @pl.when(i == 0)
    def _():
        lane = lax.broadcasted_iota(jnp.int32, (width, LANES), 1)
        km = jnp.zeros((width, LANES), F32)
        for n in range(nb):
            col = jnp.sum(kt_ref[:, n * blk:(n + 1) * blk], axis=1, keepdims=True) * (1.0 / blk)
            km = jnp.where(lane == n, col, km)
        kmt_ref[...] = km
        tq = lax.broadcasted_iota(jnp.int32, (blk, blk), 1)
        tk = lax.broadcasted_iota(jnp.int32, (blk, blk), 0)
        d0 = (tq - tk).astype(F32)
        for h in range(nh):
            sd = (slopes_ref[nh * g + h] * LOG2E) * d0
            sd_ref[h] = sd
            sdo_ref[h] = jnp.where(d0 >= 0.0, sd, -NEG)

    qt = q_ref[...].T
    qts = _bf(qt * (QK_SCALE * LOG2E))

    def scores(h, start):
        r0 = h * HEAD_DIM
        kb = kt_ref[r0:r0 + HEAD_DIM, pl.ds(start, blk)]
        return _dot_tn(kb, qts[r0:r0 + HEAD_DIM])

    for h in range(nh):
        r0 = h * HEAD_DIM
        gt = _dot3_tn(kmt_ref[r0:r0 + HEAD_DIM, :], qt[r0:r0 + HEAD_DIM, :])[0:nb]
        sel_ref[h] = _topk_rows(gt, nb, i)

    def head_step(h, carry, start, sp, shift):
        m_run, l_run, acc = carry
        r0 = h * HEAD_DIM
        m_new = jnp.maximum(m_run, jnp.max(sp, axis=0, keepdims=True) - shift)
        alpha = jnp.exp2(m_run - m_new)
        pt = jnp.exp2(sp - (m_new + shift))
        l_new = alpha * l_run + jnp.sum(pt, axis=0, keepdims=True)
        vb = vt_ref[r0:r0 + HEAD_DIM, pl.ds(start, blk)]
        acc_new = alpha * acc + _dot(vb, pt)
        return m_new, l_new, acc_new

    def past(n, carries):
        start = pl.multiple_of(n * blk, blk)
        nxt = pl.multiple_of((n + 1) * blk, blk)
        gap = ((i - n) * blk).astype(F32) * LOG2E
        out = []
        for h in range(nh):
            sp = jnp.where(sel_ref[h, pl.ds(n, 1), :] > 0.5, raw_ref[h] - sd_ref[h], NEG)
            raw_ref[h] = scores(h, nxt)
            out.append(head_step(h, carries[h], start, sp, slopes_ref[nh * g + h] * gap))
        return tuple(out)

    init = tuple((jnp.full((1, blk), NEG, F32), jnp.zeros((1, blk), F32), jnp.zeros((HEAD_DIM, blk), F32))
                 for _ in range(nh))
    for h in range(nh):
        raw_ref[h] = scores(h, 0)
    carries = lax.fori_loop(0, i, past, init)
    start = pl.multiple_of(i * blk, blk)
    outs = []
    for h in range(nh):
        _, l_fin, acc = head_step(h, carries[h], start, raw_ref[h] - sdo_ref[h], 0.0)
        outs.append(acc / l_fin)
    o_ref[...] = jnp.concatenate(outs, axis=0).T * g_ref[...]


def _moba_prompt(q, kt, vt, gate, slopes):
    b, t, _ = q.shape
    nb = t // MOBA_BLOCK
    width = HEADS_PER_STEP * HEAD_DIM
    qspec = pl.BlockSpec((None, MOBA_BLOCK, width), lambda bi, g, i: (bi, i, g))
    kspec = pl.BlockSpec((None, width, t), lambda bi, g, i: (bi, g, 0))
    return pl.pallas_call(
        functools.partial(_moba_prompt_kernel, nb),
        grid=(b, N_HEADS_A // HEADS_PER_STEP, nb),
        in_specs=[pl.BlockSpec(memory_space=pltpu.SMEM), qspec, kspec, kspec, qspec],
        out_specs=qspec,
        out_shape=jax.ShapeDtypeStruct(q.shape, F32),
        scratch_shapes=[pltpu.VMEM((width, LANES), F32), pltpu.VMEM((HEADS_PER_STEP, nb, MOBA_BLOCK), F32),
                        pltpu.VMEM((HEADS_PER_STEP, MOBA_BLOCK, MOBA_BLOCK), F32),
                        pltpu.VMEM((HEADS_PER_STEP, MOBA_BLOCK, MOBA_BLOCK), F32),
                        pltpu.VMEM((HEADS_PER_STEP, MOBA_BLOCK, MOBA_BLOCK), F32)],
        compiler_params=_params("arbitrary", "arbitrary", "arbitrary"),
        name="moba_prompt",
    )(slopes, q, kt, vt, gate)


SCORE_ROWS = 64


def _moba_sample_kernel(n_pages, page, pt_ref, q_ref, kn_ref, vn_ref, g_ref, rowc_ref, hm_ref, *rest):
    kp_refs = rest[:n_pages]
    vp_refs = rest[n_pages:2 * n_pages]
    o_ref = rest[2 * n_pages]
    s_ref = rest[2 * n_pages + 1]
    del pt_ref
    t_new = q_ref.shape[0]
    nh = N_HEADS_A
    rows = SCORE_ROWS
    pages_per_blk = MOBA_BLOCK // page
    nblk = n_pages // pages_per_blk
    hm = hm_ref[...]
    q = q_ref[...]
    qrep = jnp.concatenate([q] * (rows // t_new), axis=0) * hm
    qsb = _bf(qrep * QK_SCALE)
    slope = rowc_ref[:, 0:1]
    qpos = rowc_ref[:, 1:2]
    lane_f = lax.broadcasted_iota(jnp.int32, (rows, page), 1).astype(F32)

    ksum = [None] * nblk
    for j in range(n_pages):
        kp = kp_refs[j][...].reshape(nh * HEAD_DIM, page)
        n = j // pages_per_blk
        ksum[n] = kp if ksum[n] is None else ksum[n] + kp
        st = jnp.dot(qsb, _bf(kp), preferred_element_type=F32)
        dist = (qpos - float(j * page)) - lane_f
        s_ref[:, j * page:(j + 1) * page] = st - slope * dist

    lane_k = lax.broadcasted_iota(jnp.int32, (nh * HEAD_DIM, LANES), 1)
    ks = jnp.zeros((nh * HEAD_DIM, LANES), F32)
    for n in range(nblk):
        ks = jnp.where(lane_k == n, jnp.sum(ksum[n], axis=1, keepdims=True), ks)
    gates = _dot3(qrep, ks)
    lane_g = lax.broadcasted_iota(jnp.int32, (rows, LANES), 1)
    cnt = jnp.zeros((rows, LANES), jnp.int32)
    for m in range(nblk):
        col = gates[:, m:m + 1]
        cnt = cnt + ((col > gates) | ((col == gates) & (m < lane_g))).astype(jnp.int32)
    sel_t = (cnt < MOBA_TOPK).astype(F32)

    s_own = _dot_nt(qsb, kn_ref[...])
    trow = lax.broadcasted_iota(jnp.int32, (rows, t_new), 0) % t_new
    tcol = lax.broadcasted_iota(jnp.int32, (rows, t_new), 1)
    dist_own = (trow - tcol).astype(F32)
    s_own = jnp.where(dist_own >= 0.0, s_own - slope * dist_own, NEG)

    mvec = jnp.full((rows, page), NEG, F32)
    for j in range(n_pages):
        n = j // pages_per_blk
        st = jnp.where(sel_t[:, n:n + 1] > 0.5, s_ref[:, j * page:(j + 1) * page], NEG)
        s_ref[:, j * page:(j + 1) * page] = st
        mvec = jnp.maximum(mvec, st)
    m_row = jnp.maximum(jnp.max(mvec, axis=1, keepdims=True), jnp.max(s_own, axis=1, keepdims=True))

    p_own = jnp.exp(s_own - m_row)
    lvec = jnp.zeros((rows, page), F32)
    acc = _dot(p_own, vn_ref[...])
    for j in range(n_pages):
        pj = jnp.exp(s_ref[:, j * page:(j + 1) * page] - m_row)
        lvec = lvec + pj
        acc = acc + _dot_nt(pj, vp_refs[j][...].reshape(nh * HEAD_DIM, page))
    l_row = jnp.sum(lvec, axis=1, keepdims=True) + jnp.sum(p_own, axis=1, keepdims=True)
    tot = acc * (1.0 / l_row) * hm
    out = tot[0:t_new]
    for h in range(1, nh):
        out = out + tot[h * t_new:(h + 1) * t_new]
    o_ref[...] = out * g_ref[...]


def _moba_sample(q, k_new, v_new, gate, cache_kt, cache_vt, pt_flat, layer, consts, n_seq, t_new, n_pages):
    page = cache_kt.shape[-1]
    rowc, hm = consts
    tspec = pl.BlockSpec((t_new, D_A), lambda b, pt: (b, 0))

    def page_spec(j):
        return pl.BlockSpec((None, None, N_HEADS_A, HEAD_DIM, page),
                            lambda b, pt, j=j: (pt[b * n_pages + j], layer, 0, 0, 0))

    cspec = lambda a: pl.BlockSpec(a.shape, lambda b, pt: (0,) * a.ndim)
    grid_spec = pltpu.PrefetchScalarGridSpec(
        num_scalar_prefetch=1,
        grid=(n_seq,),
        in_specs=[tspec, tspec, tspec, tspec, cspec(rowc), cspec(hm)]
        + [page_spec(j) for j in range(n_pages)] + [page_spec(j) for j in range(n_pages)],
        out_specs=tspec,
        scratch_shapes=[pltpu.VMEM((SCORE_ROWS, n_pages * page), F32)],
    )
    return pl.pallas_call(
        functools.partial(_moba_sample_kernel, n_pages, page),
        grid_spec=grid_spec,
        out_shape=jax.ShapeDtypeStruct(q.shape, F32),
        compiler_params=_params("arbitrary"),
        name="moba_sample",
    )(pt_flat, q, k_new, v_new, gate, rowc, hm, *([cache_kt] * n_pages), *([cache_vt] * n_pages))


def _moba_sample_consts(t_new, past_len):
    nh = N_HEADS_A
    used = nh * t_new
    r = np.arange(SCORE_ROWS)
    live = r < used
    slopes = 2.0 ** (-8.0 * (np.arange(nh) + 1) / nh)
    rowc = np.zeros((SCORE_ROWS, 2), np.float32)
    rowc[:, 0] = np.where(live, slopes[np.minimum(r // t_new, nh - 1)], 0.0)
    rowc[:, 1] = past_len + (r % t_new)
    c = np.arange(nh * HEAD_DIM)
    hm = (((c[None, :] // HEAD_DIM) == (r[:, None] // t_new)) & live[:, None]).astype(np.float32)
    return jnp.asarray(rowc), jnp.asarray(hm)


def _hgrn_consts(c):
    levels = int(round(math.log2(c)))
    t = np.arange(c)[:, None]
    u = np.arange(c)[None, :]
    mats = [u <= t]
    masks = [t == u]
    for j in range(1, levels + 1):
        p = 2 ** j
        hlf = p // 2
        mid = (t // p) * p + hlf
        upper = (t % p) >= hlf
        mats.append((upper & (u >= mid) & (u <= t)) | ((~upper) & (u > t) & (u <= mid - 1)))
        masks.append(((t // p) == (u // p)) & upper & ((u % p) < hlf))
    w_all = jnp.asarray(np.concatenate(mats, axis=0), dtype=BF16)
    mk = jnp.asarray(np.stack(masks), dtype=F32)
    return w_all, mk


def _hgrn_kernel(c, levels, q_ref, lf_ref, kk_ref, v_ref, g_ref, ng_ref, w_ref, mk_ref, e_ref,
                 o_ref, sn_ref, sbd_ref):
    nh = N_HEADS_B
    width = nh * HEAD_DIM
    lf = lf_ref[...]
    d_all = _dot_sel(w_ref[...], lf, parts=2)
    b = d_all[0:c]
    e_b = jnp.exp(b)
    e_end = jnp.exp(b[c - 1:c, :] - b)
    q = q_ref[...]
    k = kk_ref[...]
    v = v_ref[...]
    vb = _bf(v)
    sbd = sbd_ref[...]
    o = _dot(q * e_b, sbd)
    lane = lax.broadcasted_iota(jnp.int32, (c, width), 1)
    qk_levels = [(q, k)]
    for j in range(1, levels + 1):
        e_j = jnp.exp(d_all[j * c:(j + 1) * c])
        qk_levels.append((q * e_j, k * e_j))
    heads = [(lane >= h * HEAD_DIM) & (lane < (h + 1) * HEAD_DIM) for h in range(nh)]
    kbs = [_bf(kj) for _, kj in qk_levels]
    a_heads = []
    for h in range(nh):
        a = jnp.zeros((c, c), F32)
        for j, (qj, _) in enumerate(qk_levels):
            a = a + _dot_nt(jnp.where(heads[h], qj, 0.0), kbs[j]) * mk_ref[j]
        a_heads.append(_bf(a))
    for h in range(nh):
        o = o + jnp.where(heads[h], jnp.dot(a_heads[h], vb, preferred_element_type=F32), 0.0)

    ones = jnp.ones((c, LANES), BF16)
    hi, mid, lo = _split3(lf)
    dn = (((0,), (0,)), ((), ()))
    colsum = (lax.dot_general(hi, ones, dn, preferred_element_type=F32)
              + lax.dot_general(mid, ones, dn, preferred_element_type=F32)
              + lax.dot_general(lo, ones, dn, preferred_element_type=F32))
    decay = jnp.exp(colsum)
    decay = jnp.concatenate([decay] * (width // LANES), axis=1)
    r = lax.broadcasted_iota(jnp.int32, (width, width), 0) // HEAD_DIM
    cc = lax.broadcasted_iota(jnp.int32, (width, width), 1) // HEAD_DIM
    s_new = sbd * decay + jnp.where(r == cc, _dot_tn(k * e_end, v), 0.0)
    sbd_ref[...] = s_new

    ss = _seg_sum(o * o, e_ref)
    o_ref[...] = o * lax.rsqrt(ss * (1.0 / HEAD_DIM) + EPS) * ng_ref[...] * g_ref[...]
    for h in range(nh):
        sn_ref[h] = s_new[h * HEAD_DIM:(h + 1) * HEAD_DIM, h * HEAD_DIM:(h + 1) * HEAD_DIM]


def _mlstm_consts(c):
    t = np.arange(c)[:, None]
    u = np.arange(c)[None, :]
    return jnp.asarray(u <= t, dtype=BF16)


def _mlstm_kernel(c, uc_ref, g_ref, cw_ref, cb_ref, wq_ref, wk_ref, wv_ref,
                  wg_ref, bg_ref, skip_ref, ng_ref, tri_ref, e_ref,
                  o_ref, cn_ref, nn_ref, mn_ref, cvn_ref,
                  uext_ref, cbd_ref, n_ref, m_ref):
    nh = N_HEADS_C
    width = nh * HEAD_DIM

    uc = uc_ref[...]
    uext_ref[8:8 + c, :] = uc
    conv = cb_ref[...]
    for j in range(CONV_W):
        conv = conv + cw_ref[j:j + 1, :] * uext_ref[5 + j:5 + j + c, :]
    tail = uext_ref[c:c + 8, :]
    uext_ref[0:8, :] = tail
    uconv = _silu(conv)

    qm = _dot(uconv, wq_ref[...])
    km = _dot(uconv, wk_ref[...])
    vm = _dot(uc, wv_ref[...])
    i_raw = _dot3(qm, wg_ref[0]) + _dot3(km, wg_ref[1]) + _dot3(vm, wg_ref[2]) + bg_ref[...]
    logf = pltpu.roll(_log_sigmoid(i_raw), GATE_LANES - N_HEADS_C, 1)
    bcum = _dot_sel(tri_ref[...], logf)
    a = i_raw - bcum
    rowi = lax.broadcasted_iota(jnp.int32, (c, GATE_LANES), 0)
    s = 1
    while s < c:
        a = jnp.maximum(a, jnp.where(rowi >= s, pltpu.roll(a, s, 0), -jnp.inf))
        s *= 2
    m0 = m_ref[...]
    m_t = bcum + jnp.maximum(m0, a)
    g_in = jnp.exp(bcum + m0 - m_t)
    bm = bcum - m_t
    ib_t = (i_raw - bcum).T
    m_end = m_t[c - 1:c, :]
    b_end = bcum[c - 1:c, :]
    w_tok = jnp.exp((b_end - bcum) + i_raw - m_end)
    g_end = jnp.exp(b_end + m0 - m_end)

    ks = km * QK_SCALE
    ksb = _bf(ks)
    vmb = _bf(vm)
    lane = lax.broadcasted_iota(jnp.int32, (c, width), 1)
    trow = lax.broadcasted_iota(jnp.int32, (c, c), 0)
    tcol = lax.broadcasted_iota(jnp.int32, (c, c), 1)
    causal = tcol <= trow
    g256 = _spread_heads(g_in, c, nh)
    cbd = cbd_ref[...]
    n0 = n_ref[...]
    num = g256 * _dot(qm, cbd)
    qn = qm * n0
    den_cols = jnp.zeros((c, GATE_LANES), F32)
    lane_g = lax.broadcasted_iota(jnp.int32, (c, GATE_LANES), 1)
    heads = [(lane >= h * HEAD_DIM) & (lane < (h + 1) * HEAD_DIM) for h in range(nh)]
    raw = [_dot_nt(jnp.where(heads[h], qm, 0.0), ksb) for h in range(nh)]
    for h in range(nh):
        expo = bm[:, h:h + 1] + ib_t[h:h + 1, :]
        dmat = jnp.exp(jnp.where(causal, expo, NEG))
        qk = raw[h] * dmat
        num = num + jnp.where(heads[h], jnp.dot(_bf(qk), vmb, preferred_element_type=F32), 0.0)
        den_h = (g_in[:, h:h + 1] * jnp.sum(jnp.where(heads[h], qn, 0.0), axis=1, keepdims=True)
                 + jnp.sum(qk, axis=1, keepdims=True))
        den_cols = jnp.where(lane_g == h, den_h, den_cols)
    denom = jnp.maximum(jnp.abs(den_cols), jnp.exp(-m_t))
    hval = num / _spread_heads(denom, c, nh)

    w256 = _spread_heads(w_tok, c, nh)
    gend256 = _spread_heads(g_end, 1, nh)
    r = lax.broadcasted_iota(jnp.int32, (width, width), 0) // HEAD_DIM
    cc = lax.broadcasted_iota(jnp.int32, (width, width), 1) // HEAD_DIM
    kw = ks * w256
    c_new = cbd * gend256 + jnp.where(r == cc, _dot_tn(kw, vm), 0.0)
    n_new = gend256 * n0 + jnp.sum(kw, axis=0, keepdims=True)
    cbd_ref[...] = c_new
    n_ref[...] = n_new
    m_ref[...] = m_end

    mean = _seg_sum(hval, e_ref) * (1.0 / HEAD_DIM)
    xc = hval - mean
    var = _seg_sum(xc * xc, e_ref) * (1.0 / HEAD_DIM)
    hc = xc * lax.rsqrt(var + EPS) * ng_ref[...]
    o_ref[...] = (hc + skip_ref[...] * uconv) * g_ref[...]
    for h in range(nh):
        cn_ref[h] = c_new[h * HEAD_DIM:(h + 1) * HEAD_DIM, h * HEAD_DIM:(h + 1) * HEAD_DIM]
    nn_ref[...] = n_new
    mn_ref[...] = m_end
    cvn_ref[...] = tail


N_HGRN_IN, N_HGRN_OUT, N_HGRN_SCRATCH = 9, 2, 1
N_MLSTM_IN, N_MLSTM_OUT, N_MLSTM_SCRATCH = 13, 5, 4


def _recur_kernel(c, levels, *refs):
    i0 = 0
    hg_in = refs[i0:i0 + N_HGRN_IN]
    i0 += N_HGRN_IN
    ml_in = refs[i0:i0 + N_MLSTM_IN]
    i0 += N_MLSTM_IN
    hg_out = refs[i0:i0 + N_HGRN_OUT]
    i0 += N_HGRN_OUT
    ml_out = refs[i0:i0 + N_MLSTM_OUT]
    i0 += N_MLSTM_OUT
    hg_scr = refs[i0:i0 + N_HGRN_SCRATCH]
    i0 += N_HGRN_SCRATCH
    ml_scr = refs[i0:i0 + N_MLSTM_SCRATCH]

    @pl.when(pl.program_id(1) == 0)
    def _():
        for ref in hg_scr + ml_scr:
            ref[...] = jnp.zeros(ref.shape, F32)

    _hgrn_kernel(c, levels, *hg_in, *hg_out, *hg_scr)
    _mlstm_kernel(c, *ml_in, *ml_out, *ml_scr)


def _recur_prompt(qb, lf, kk, ib, gb, uc, gc, lw, shared, c):
    b, t, width = qb.shape
    w_all, mk = shared['hgrn_consts']
    levels = mk.shape[0] - 1
    tok = pl.BlockSpec((None, c, width), lambda bi, ci: (bi, ci, 0))
    per_b = lambda shp: pl.BlockSpec((None,) + shp, lambda bi, ci: (bi,) + (0,) * len(shp))
    hg_w = [lw['hng'], w_all, mk, shared['e256']]
    ml_w = [lw['conv_w'], lw['conv_b'], lw['wq'], lw['wk'], lw['wv'], lw['wg'], lw['bg'],
            lw['skip'], lw['mng'], shared['mlstm_tri'], shared['e256']]
    assert 5 + len(hg_w) == N_HGRN_IN and 2 + len(ml_w) == N_MLSTM_IN
    state = (N_HEADS_B, HEAD_DIM, HEAD_DIM)
    out_shape = [jax.ShapeDtypeStruct(qb.shape, F32), jax.ShapeDtypeStruct((b,) + state, F32),
                 jax.ShapeDtypeStruct(uc.shape, F32), jax.ShapeDtypeStruct((b,) + state, F32),
                 jax.ShapeDtypeStruct((b, 1, width), F32), jax.ShapeDtypeStruct((b, 1, GATE_LANES), F32),
                 jax.ShapeDtypeStruct((b, SUBLANES, width), F32)]
    return pl.pallas_call(
        functools.partial(_recur_kernel, c, levels),
        grid=(b, t // c),
        in_specs=[tok] * 5 + [_const_spec(w.shape) for w in hg_w] + [tok] * 2 + [_const_spec(w.shape) for w in ml_w],
        out_specs=[tok, per_b(state), tok, per_b(state), per_b((1, width)), per_b((1, GATE_LANES)),
                   per_b((SUBLANES, width))],
        out_shape=out_shape,
        scratch_shapes=[pltpu.VMEM((width, width), F32),
                        pltpu.VMEM((c + SUBLANES, width), F32), pltpu.VMEM((width, width), F32),
                        pltpu.VMEM((1, width), F32), pltpu.VMEM((1, GATE_LANES), F32)],
        compiler_params=_params("arbitrary", "arbitrary"),
        name="recur_prompt",
    )(qb, lf, kk, ib, gb, *hg_w, uc, gc, *ml_w)


K_UNROLL = 8


def _hgrn_sample_kernel(t_new, q_ref, kk_ref, v_ref, g_ref, s0_ref, ng_ref, o_ref, sn_ref):
    sn_ref[...] = s0_ref[...]
    for t in range(t_new):
        vt = v_ref[t]

        def kbody(kb, o, t=t):
            for kk in range(K_UNROLL):
                k = kb * K_UNROLL + kk
                kt = kk_ref[t, pl.ds(k, 1), :]
                s_k = (1.0 - kt) * sn_ref[k] + kt * vt
                sn_ref[k] = s_k
                o = o + s_k * q_ref[t, pl.ds(k, 1), :]
            return o

        o = lax.fori_loop(0, HEAD_DIM // K_UNROLL, kbody, jnp.zeros(vt.shape, F32))
        ss = jnp.sum(o * o, axis=0, keepdims=True) * (1.0 / HEAD_DIM)
        o_ref[t] = o * lax.rsqrt(ss + EPS) * ng_ref[...] * g_ref[t]


def _hgrn_sample(qt, kkt, vt, gt, s0, ng_col):
    t_new, width, b = qt.shape
    tok = pl.BlockSpec((t_new, HEAD_DIM, b), lambda h: (0, h, 0))
    st = pl.BlockSpec((None, HEAD_DIM, HEAD_DIM, b), lambda h: (h, 0, 0, 0))
    return pl.pallas_call(
        functools.partial(_hgrn_sample_kernel, t_new),
        grid=(N_HEADS_B,),
        in_specs=[tok, tok, tok, tok, st, pl.BlockSpec((HEAD_DIM, 1), lambda h: (h, 0))],
        out_specs=[tok, st],
        out_shape=[jax.ShapeDtypeStruct(qt.shape, F32), jax.ShapeDtypeStruct(s0.shape, F32)],
        compiler_params=_params("arbitrary"),
        name="hgrn_sample",
    )(qt, kkt, vt, gt, s0, ng_col)


def _mlstm_front_kernel(t_new, uc_ref, cv0_ref, cw_ref, cb_ref, wq_ref, wk_ref, wv_ref, wi_ref, wf_ref,
                        bi_ref, bf_ref, uconv_ref, qm_ref, km_ref, vm_ref, i_ref, lf_ref):
    hist = [cv0_ref[j] for j in range(CONV_W - 1)] + [uc_ref[t] for t in range(t_new)]
    for t in range(t_new):
        conv = cb_ref[...]
        for j in range(CONV_W):
            conv = conv + cw_ref[j] * hist[t + j]
        uconv = _silu(conv)
        uconv_ref[t] = uconv
        qm = _dot(wq_ref[...], uconv)
        km = _dot(wk_ref[...], uconv)
        vm = _dot(wv_ref[...], hist[t + CONV_W - 1])
        qm_ref[t] = qm
        km_ref[t] = km
        vm_ref[t] = vm
        i_ref[t] = _dot3(wi_ref[0], qm) + _dot3(wi_ref[1], km) + _dot3(wi_ref[2], vm) + bi_ref[...]
        lf_ref[t] = _log_sigmoid(_dot3(wf_ref[0], qm) + _dot3(wf_ref[1], km) + _dot3(wf_ref[2], vm)
                                 + bf_ref[...])


def _mlstm_front(uct, cv0t, lw):
    t_new, width, b = uct.shape
    weights = [lw['conv_w_col'], lw['conv_b_col'], lw['wq_t'], lw['wk_t'], lw['wv_t'], lw['wi_t'], lw['wf_t'],
               lw['bi_col'], lw['bf_col']]
    big = jax.ShapeDtypeStruct(uct.shape, F32)
    small = jax.ShapeDtypeStruct((t_new, SUBLANES, b), F32)
    return pl.pallas_call(
        functools.partial(_mlstm_front_kernel, t_new),
        grid=(1,),
        in_specs=[_const_spec(uct.shape), _const_spec(cv0t.shape)] + [_const_spec(w.shape) for w in weights],
        out_specs=[_const_spec(uct.shape)] * 4 + [_const_spec(small.shape)] * 2,
        out_shape=[big] * 4 + [small] * 2,
        compiler_params=_params("arbitrary"),
        name="mlstm_front",
    )(uct, cv0t, *weights)


def _mlstm_sample_kernel(t_new, qm_ref, km_ref, vm_ref, i_ref, lf_ref, uconv_ref, g_ref, c0_ref, n0_ref, m0_ref,
                         skip_ref, ng_ref, o_ref, cn_ref, nn_ref, mn_ref):
    h = pl.program_id(0)
    cn_ref[...] = c0_ref[...]
    n = n0_ref[...]
    m = m0_ref[...]
    for t in range(t_new):
        i_t = i_ref[t, pl.ds(h, 1), :]
        lf_t = lf_ref[t, pl.ds(h, 1), :]
        m_new = jnp.maximum(lf_t + m, i_t)
        fp = jnp.exp(lf_t + m - m_new)
        ip = jnp.exp(i_t - m_new)
        vt = vm_ref[t]
        n = fp * n + ip * (km_ref[t] * QK_SCALE)

        def kbody(kb, num, t=t, fp=fp, ip=ip, vt=vt):
            for kk in range(K_UNROLL):
                k = kb * K_UNROLL + kk
                kt = km_ref[t, pl.ds(k, 1), :] * QK_SCALE
                c_k = fp * cn_ref[k] + (ip * kt) * vt
                cn_ref[k] = c_k
                num = num + c_k * qm_ref[t, pl.ds(k, 1), :]
            return num

        num = lax.fori_loop(0, HEAD_DIM // K_UNROLL, kbody, jnp.zeros(vt.shape, F32))
        den = jnp.sum(qm_ref[t] * n, axis=0, keepdims=True)
        hval = num / jnp.maximum(jnp.abs(den), jnp.exp(-m_new))
        m = m_new
        mean = jnp.sum(hval, axis=0, keepdims=True) * (1.0 / HEAD_DIM)
        xc = hval - mean
        var = jnp.sum(xc * xc, axis=0, keepdims=True) * (1.0 / HEAD_DIM)
        hc = xc * lax.rsqrt(var + EPS) * ng_ref[...]
        o_ref[t] = (hc + skip_ref[...] * uconv_ref[t]) * g_ref[t]
    nn_ref[...] = n
    mn_ref[...] = m


def _mlstm_sample(front, gt, c0, n0, m0, lw):
    uconv, qm, km, vm, i_raw, logf = front
    t_new, width, b = qm.shape
    tok = pl.BlockSpec((t_new, HEAD_DIM, b), lambda h: (0, h, 0))
    gates = _const_spec(i_raw.shape)
    st = pl.BlockSpec((None, HEAD_DIM, HEAD_DIM, b), lambda h: (h, 0, 0, 0))
    nst = pl.BlockSpec((None, HEAD_DIM, b), lambda h: (h, 0, 0))
    mst = pl.BlockSpec((None, 1, b), lambda h: (h, 0, 0))
    col = pl.BlockSpec((HEAD_DIM, 1), lambda h: (h, 0))
    return pl.pallas_call(
        functools.partial(_mlstm_sample_kernel, t_new),
        grid=(N_HEADS_C,),
        in_specs=[tok, tok, tok, gates, gates, tok, tok, st, nst, mst, col, col],
        out_specs=[tok, st, nst, mst],
        out_shape=[jax.ShapeDtypeStruct(qm.shape, F32), jax.ShapeDtypeStruct(c0.shape, F32),
                   jax.ShapeDtypeStruct(n0.shape, F32), jax.ShapeDtypeStruct(m0.shape, F32)],
        compiler_params=_params("arbitrary"),
        name="mlstm_sample",
    )(qm, km, vm, i_raw, logf, uconv, gt, c0, n0, m0, lw['skip_col'], lw['mng_col'])


def _gate_weights(w_i, w_f):
    w3 = jnp.concatenate([w_i.reshape(3, D_C, N_HEADS_C), w_f.reshape(3, D_C, N_HEADS_C)], axis=-1)
    return jnp.pad(w3, ((0, 0), (0, 0), (0, GATE_LANES - 2 * N_HEADS_C)))


def _gate_weights_t(w):
    w3 = jnp.swapaxes(w.reshape(3, D_C, N_HEADS_C), 1, 2)
    return jnp.pad(w3, ((0, 0), (0, SUBLANES - N_HEADS_C), (0, 0)))


def _pad_lanes(v):
    return jnp.pad(v, ((0, 0),) * (v.ndim - 1) + ((0, GATE_LANES - v.shape[-1]),))


def _pad_rows_col(v):
    return jnp.pad(v, (0, SUBLANES - v.shape[0]))[:, None]


def _layer_weights(l, norm_g, w_in, q_norm_g, k_norm_g, hgrn_norm_g, mlstm_conv_w, mlstm_conv_b, mlstm_wq,
                   mlstm_wk, mlstm_wv, mlstm_w_ig, mlstm_b_ig, mlstm_w_fg, mlstm_b_fg, mlstm_skip, mlstm_norm_g,
                   w_out):
    bd = lambda w: _block_diag(w).astype(BF16)
    bdt = lambda w: _block_diag(jnp.swapaxes(w, 1, 2)).astype(BF16)
    return {
        'norm_g': norm_g[l][None, :],
        'w_in': w_in[l].astype(BF16),
        'qg': jnp.tile(q_norm_g[l], N_HEADS_A)[None, :],
        'kg': jnp.tile(k_norm_g[l], N_HEADS_A)[None, :],
        'hng': hgrn_norm_g[l][None, :],
        'hng_col': hgrn_norm_g[l][:, None],
        'conv_w': mlstm_conv_w[l],
        'conv_b': mlstm_conv_b[l][None, :],
        'conv_w_col': mlstm_conv_w[l][:, :, None],
        'conv_b_col': mlstm_conv_b[l][:, None],
        'wq': bd(mlstm_wq[l]), 'wk': bd(mlstm_wk[l]), 'wv': bd(mlstm_wv[l]),
        'wq_t': bdt(mlstm_wq[l]), 'wk_t': bdt(mlstm_wk[l]), 'wv_t': bdt(mlstm_wv[l]),
        'wg': _gate_weights(mlstm_w_ig[l], mlstm_w_fg[l]),
        'wi_t': _gate_weights_t(mlstm_w_ig[l]), 'wf_t': _gate_weights_t(mlstm_w_fg[l]),
        'bg': _pad_lanes(jnp.concatenate([mlstm_b_ig[l], mlstm_b_fg[l]])[None, :]),
        'bi_col': _pad_rows_col(mlstm_b_ig[l]), 'bf_col': _pad_rows_col(mlstm_b_fg[l]),
        'skip': mlstm_skip[l][None, :], 'mng': mlstm_norm_g[l][None, :],
        'skip_col': mlstm_skip[l][:, None], 'mng_col': mlstm_norm_g[l][:, None],
        'wo_a': w_out[l][:D_A].astype(BF16),
        'wo_b': w_out[l][D_A:D_A + D_B].astype(BF16),
        'wo_c': w_out[l][D_A + D_B:].astype(BF16),
    }


def _prompt_layer(x, layer, lw, shared, chunk, tm):
    b, t, d = x.shape
    x2d = x.reshape(b * t, d)
    (qa, kt, vt, ga, qb, lf, kk, ib, gb, uc, gc) = _inproj(x2d, layer, lw, shared, tm, seq_len=t)
    r3 = lambda a: a.reshape(b, t, a.shape[-1])
    oa = _moba_prompt(r3(qa), kt, vt, r3(ga), shared['slopes'])
    ob, s_new, oc, c_new, n_new, m_new, cv_new = _recur_prompt(r3(qb), r3(lf), r3(kk), r3(ib), r3(gb), r3(uc),
                                                               r3(gc), lw, shared, chunk)
    y = _outproj(x2d, oa.reshape(b * t, D_A), ob.reshape(b * t, D_B), oc.reshape(b * t, D_C), lw, tm)
    return (y.reshape(b, t, d), kt, vt, s_new, c_new, n_new.reshape(b, N_HEADS_C, HEAD_DIM),
            m_new[:, 0, :N_HEADS_C], cv_new[:, SUBLANES - (CONV_W - 1):, :])


def _sample_layer(x, layer, lw, shared, cache_kt, cache_vt, pt_flat, s0, c0, n0, m0, cv0t, tm):
    b, t, d = x.shape
    n_pages = pt_flat.shape[0] // b
    x2d = x.reshape(b * t, d)
    (qa, ka, va, ga, qb, lf, kk, ib, gb, uc, gc) = _inproj(x2d, layer, lw, shared, tm)
    del lf
    oa = _moba_sample(qa, ka, va, ga, cache_kt, cache_vt, pt_flat, layer, shared['sample_consts'], b, t, n_pages)
    to_lanes = lambda a: jnp.transpose(a.reshape(b, t, a.shape[-1]), (1, 2, 0))
    from_lanes = lambda a: jnp.transpose(a, (2, 0, 1)).reshape(b * t, a.shape[1])
    obt, s_new = _hgrn_sample(to_lanes(qb), to_lanes(kk), to_lanes(ib), to_lanes(gb), s0, lw['hng_col'])
    front = _mlstm_front(to_lanes(uc), cv0t, lw)
    oct, c_new, n_new, m_new = _mlstm_sample(front, to_lanes(gc), c0, n0, m0, lw)
    y = _outproj(x2d, oa, from_lanes(obt), from_lanes(oct), lw, tm)
    conv_new = uc.reshape(b, t, D_C)[:, t - (CONV_W - 1):, :]
    return (y.reshape(b, t, d), ka.reshape(b, t, N_HEADS_A, HEAD_DIM), va.reshape(b, t, N_HEADS_A, HEAD_DIM),
            s_new, c_new, n_new, m_new, conv_new)


def kernel(x_prompt, x_sample, cache_k, cache_v, page_table, state_hgrn, state_mlstm_c, state_mlstm_n,
           state_mlstm_m, state_mlstm_conv, norm_g, w_in, q_norm_g, k_norm_g, hgrn_lb, hgrn_norm_g,
           mlstm_conv_w, mlstm_conv_b, mlstm_wq, mlstm_wk, mlstm_wv, mlstm_w_ig, mlstm_b_ig, mlstm_w_fg,
           mlstm_b_fg, mlstm_skip, mlstm_norm_g, w_out):
    depth = w_in.shape[0]
    bp, tp, _ = x_prompt.shape
    bd, td, _ = x_sample.shape
    n_pages = page_table.shape[1]
    page = cache_k.shape[2]
    chunk = min(tp, 256)
    shared = {
        'e256': _head_block_ones(),
        'hgrn_lb': hgrn_lb.astype(F32),
        'hgrn_consts': _hgrn_consts(chunk),
        'mlstm_tri': _mlstm_consts(chunk),
        'slopes': jnp.asarray(2.0 ** (-8.0 * (np.arange(N_HEADS_A) + 1) / N_HEADS_A), dtype=F32),
        'sample_consts': _moba_sample_consts(td, n_pages * page),
    }
    pt_flat = page_table.reshape(-1).astype(jnp.int32)
    cache_kt = jnp.transpose(cache_k, (0, 1, 3, 4, 2))
    cache_vt = jnp.transpose(cache_v, (0, 1, 3, 4, 2))
    s_h = jnp.transpose(state_hgrn, (0, 2, 3, 4, 1))
    s_c = jnp.transpose(state_mlstm_c, (0, 2, 3, 4, 1))
    s_n = jnp.transpose(state_mlstm_n, (0, 2, 3, 1))
    s_m = jnp.transpose(state_mlstm_m, (0, 2, 1))[:, :, None, :]
    s_cv = jnp.transpose(state_mlstm_conv, (0, 2, 3, 1))

    yp, ys = x_prompt, x_sample
    outs_p = [[] for _ in range(7)]
    outs_s = [[] for _ in range(7)]
    for l in range(depth):
        lw = _layer_weights(l, norm_g, w_in, q_norm_g, k_norm_g, hgrn_norm_g, mlstm_conv_w, mlstm_conv_b,
                            mlstm_wq, mlstm_wk, mlstm_wv, mlstm_w_ig, mlstm_b_ig, mlstm_w_fg, mlstm_b_fg,
                            mlstm_skip, mlstm_norm_g, w_out)
        res_p = _prompt_layer(yp, l, lw, shared, chunk, min(PROMPT_ROW_TILE, tp))
        res_s = _sample_layer(ys, l, lw, shared, cache_kt, cache_vt, pt_flat, s_h[l], s_c[l], s_n[l], s_m[l],
                              s_cv[l], SAMPLE_ROW_TILE)
        yp, ys = res_p[0], res_s[0]
        for acc, a in zip(outs_p, res_p[1:]):
            acc.append(a)
        for acc, a in zip(outs_s, res_s[1:]):
            acc.append(a)

    st = lambda lst, ax: jnp.stack(lst, axis=ax)

    def kv_prompt(lst):
        a = st(lst, 1).reshape(bp, depth, N_HEADS_A, HEAD_DIM, tp)
        return jnp.transpose(a, (0, 1, 4, 2, 3))

    batch_first = lambda a: jnp.moveaxis(a, -1, 1)
    return (yp, ys, kv_prompt(outs_p[0]), kv_prompt(outs_p[1]), st(outs_s[0], 1), st(outs_s[1], 1),
            st(outs_p[2], 0), batch_first(st(outs_s[2], 0)), st(outs_p[3], 0), batch_first(st(outs_s[3], 0)),
            st(outs_p[4], 0), batch_first(st(outs_s[4], 0)), st(outs_p[5], 0),
            batch_first(st(outs_s[5], 0)[:, :, 0, :]), st(outs_p[6], 0), st(outs_s[6], 0))
```

```python
import functools
import math

import numpy as np
import jax
import jax.numpy as jnp
from jax import lax
from jax.experimental import pallas as pl
from jax.experimental.pallas import tpu as pltpu

F32 = jnp.float32
BF16 = jnp.bfloat16

HEAD_DIM = 64
N_HEADS_A = 8
N_HEADS_B = 4
N_HEADS_C = 4
D_A = N_HEADS_A * HEAD_DIM
D_B = N_HEADS_B * HEAD_DIM
D_C = N_HEADS_C * HEAD_DIM
MOBA_BLOCK = 256
MOBA_TOPK = 3
CONV_W = 4
EPS = 1e-6
NEG = -1e30
GATE_LANES = 128
LANES = 128
SUBLANES = 8
VMEM_LIMIT = 56 * 1024 * 1024
PROMPT_ROW_TILE = 512
SAMPLE_ROW_TILE = 256
QK_SCALE = HEAD_DIM ** -0.5
LOG2E = math.log2(math.e)


def _bf(x):
    return x.astype(BF16)


def _dot(a, b):
    return jnp.dot(_bf(a), _bf(b), preferred_element_type=F32)


def _dot_nt(a, b):
    return lax.dot_general(_bf(a), _bf(b), (((1,), (1,)), ((), ())), preferred_element_type=F32)


def _dot_tn(a, b):
    return lax.dot_general(_bf(a), _bf(b), (((0,), (0,)), ((), ())), preferred_element_type=F32)


def _split2(x):
    hi = _bf(x)
    lo = _bf(x - hi.astype(F32))
    return hi, lo


def _split3(x):
    hi = _bf(x)
    r = x - hi.astype(F32)
    mid = _bf(r)
    lo = _bf(r - mid.astype(F32))
    return hi, mid, lo


def _dot_sel(w01, x, parts=3):
    pieces = _split3(x)[:parts]
    out = jnp.dot(w01, pieces[0], preferred_element_type=F32)
    for p in pieces[1:]:
        out = out + jnp.dot(w01, p, preferred_element_type=F32)
    return out


def _dot3(a, b):
    ah, al = _split2(a)
    bh, bl = _split2(b)
    return (jnp.dot(ah, bh, preferred_element_type=F32) + jnp.dot(al, bh, preferred_element_type=F32)
            + jnp.dot(ah, bl, preferred_element_type=F32))


def _dot3_tn(a, b):
    ah, al = _split2(a)
    bh, bl = _split2(b)
    dn = (((0,), (0,)), ((), ()))
    return (lax.dot_general(ah, bh, dn, preferred_element_type=F32)
            + lax.dot_general(al, bh, dn, preferred_element_type=F32)
            + lax.dot_general(ah, bl, dn, preferred_element_type=F32))


def _seg_sum(x, e_ref):
    hi, lo = _split2(x)
    e = e_ref[...]
    return jnp.dot(hi, e, preferred_element_type=F32) + jnp.dot(lo, e, preferred_element_type=F32)


def _spread_heads(x, rows, n_heads):
    lane = lax.broadcasted_iota(jnp.int32, (rows, n_heads * HEAD_DIM), 1)
    out = jnp.zeros((rows, n_heads * HEAD_DIM), F32)
    for h in range(n_heads):
        in_head = (lane >= h * HEAD_DIM) & (lane < (h + 1) * HEAD_DIM)
        out = jnp.where(in_head, x[:, h:h + 1], out)
    return out


def _silu(x):
    return x * jax.nn.sigmoid(x)


def _log_sigmoid(x):
    return jnp.minimum(x, 0.0) - jnp.log(1.0 + jnp.exp(-jnp.abs(x)))


def _topk_rows(g, n_rows, limit):
    rid = lax.broadcasted_iota(jnp.int32, g.shape, 0)
    g = jnp.where(rid < limit, g, -jnp.inf)
    cnt = jnp.zeros(g.shape, jnp.int32)
    for m in range(n_rows):
        row = g[m:m + 1, :]
        beats = (row > g) | ((row == g) & (m < rid))
        cnt = cnt + beats.astype(jnp.int32)
    return ((cnt < MOBA_TOPK) & (rid < limit)).astype(F32)


def _const_spec(shape):
    nd = len(shape)
    return pl.BlockSpec(shape, lambda *_: (0,) * nd)


def _head_block_ones():
    r = np.arange(256)
    return jnp.asarray((r[:, None] // HEAD_DIM) == (r[None, :] // HEAD_DIM), dtype=BF16)


def _block_diag(w):
    h = w.shape[0]
    eye = jnp.eye(h, dtype=w.dtype)
    return jnp.einsum('hde,hg->hdge', w, eye).reshape(h * HEAD_DIM, h * HEAD_DIM)


def _params(*sem):
    return pltpu.CompilerParams(dimension_semantics=sem, vmem_limit_bytes=VMEM_LIMIT)


def _inproj_kernel(layer, kv_transposed, x_ref, g_ref, w_ref, qg_ref, kg_ref, lb_ref, e_ref,
                   qa_ref, ka_ref, va_ref, ga_ref, qb_ref, lf_ref, kk_ref, ib_ref, gb_ref, uc_ref, gc_ref):
    x = x_ref[...]
    h = x * lax.rsqrt(jnp.mean(x * x, axis=-1, keepdims=True) + EPS) * g_ref[...]
    hb = _bf(h)

    def proj(c0, width):
        return jnp.dot(hb, w_ref[:, c0:c0 + width], preferred_element_type=F32)

    def head_rms(p, g):
        halves = [_seg_sum(p[:, c:c + 256] * p[:, c:c + 256], e_ref) for c in (0, 256)]
        ss = jnp.concatenate(halves, axis=1)
        return p * lax.rsqrt(ss * (1.0 / HEAD_DIM) + EPS) * g

    qa_ref[...] = head_rms(proj(0, D_A), qg_ref[...])
    ka = head_rms(proj(D_A, D_A), kg_ref[...])
    va = proj(2 * D_A, D_A)
    if kv_transposed:
        ka_ref[...] = ka.T
        va_ref[...] = va.T
    else:
        ka_ref[...] = ka
        va_ref[...] = va
    ga_ref[...] = _silu(proj(3 * D_A, D_A))
    c = 4 * D_A
    qb_ref[...] = proj(c, D_B)
    lbp = lb_ref[...]
    lbe = jnp.exp(lbp - jnp.max(lbp, axis=0, keepdims=True))
    lbw = lbe / jnp.sum(lbe, axis=0, keepdims=True)
    lb_cum = lbw[0:1, :]
    for j in range(1, layer + 1):
        lb_cum = lb_cum + lbw[j:j + 1, :]
    lb = lb_cum - lbw[0:1, :]
    fg = lb + (1.0 - lb) * jax.nn.sigmoid(proj(c + D_B, D_B))
    lf_ref[...] = jnp.log(fg)
    kk_ref[...] = 1.0 - fg
    ib_ref[...] = proj(c + 2 * D_B, D_B)
    gb_ref[...] = _silu(proj(c + 3 * D_B, D_B))
    c = c + 4 * D_B
    uc_ref[...] = proj(c, D_C)
    gc_ref[...] = _silu(proj(c + D_C, D_C))


N_INPROJ_IN = 7


def _inproj_kernel_stacked(layer, *refs):
    _inproj_kernel(layer, True, *refs[:N_INPROJ_IN], *refs[N_INPROJ_IN + 2:])


def _inproj(x2d, layer, lw, shared, tm, seq_len=None, depth=None, kv_prev=None):
    m, d = x2d.shape
    kv_transposed = seq_len is not None
    widths = [D_A] * 4 + [D_B] * 5 + [D_C] * 2
    out_shape = [jax.ShapeDtypeStruct((m, w), F32) for w in widths]
    out_specs = [pl.BlockSpec((tm, w), lambda i: (i, 0)) for w in widths]
    if kv_transposed:
        tiles = seq_len // tm
        for idx in (1, 2):
            out_shape[idx] = jax.ShapeDtypeStruct((m // seq_len, depth, D_A, seq_len), F32)
            out_specs[idx] = pl.BlockSpec((None, None, D_A, tm), lambda i: (i // tiles, layer, 0, i % tiles))
    w_bf = lw['w_in']
    in_specs = [pl.BlockSpec((tm, d), lambda i: (i, 0)),
                _const_spec((1, d)), _const_spec(w_bf.shape), _const_spec((1, D_A)), _const_spec((1, D_A)),
                _const_spec(shared['hgrn_lb'].shape), _const_spec((256, 256))]
    args = [x2d, lw['norm_g'], w_bf, lw['qg'], lw['kg'], shared['hgrn_lb'], shared['e256']]
    assert len(args) == N_INPROJ_IN
    body = functools.partial(_inproj_kernel, layer, kv_transposed)
    aliases = {}
    if kv_prev is not None:
        in_specs += [pl.BlockSpec(memory_space=pl.ANY)] * 2
        args += list(kv_prev)
        aliases = {N_INPROJ_IN: 1, N_INPROJ_IN + 1: 2}
        body = functools.partial(_inproj_kernel_stacked, layer)
    return pl.pallas_call(
        body,
        grid=(m // tm,),
        in_specs=in_specs,
        out_specs=out_specs,
        out_shape=out_shape,
        input_output_aliases=aliases,
        compiler_params=_params("arbitrary"),
        name="inproj",
    )(*args)


def _outproj_kernel(x_ref, oa_ref, ob_ref, oc_ref, wa_ref, wb_ref, wc_ref, y_ref):
    y_ref[...] = (x_ref[...] + _dot(oa_ref[...], wa_ref[...]) + _dot(ob_ref[...], wb_ref[...])
                  + _dot(oc_ref[...], wc_ref[...]))


def _outproj(x2d, oa, ob, oc, lw, tm):
    m, d = x2d.shape
    row = lambda w: pl.BlockSpec((tm, w), lambda i: (i, 0))
    wa, wb, wc = lw['wo_a'], lw['wo_b'], lw['wo_c']
    return pl.pallas_call(
        _outproj_kernel,
        grid=(m // tm,),
        in_specs=[row(d), row(D_A), row(D_B), row(D_C), _const_spec(wa.shape), _const_spec(wb.shape),
                  _const_spec(wc.shape)],
        out_specs=row(d),
        out_shape=jax.ShapeDtypeStruct((m, d), F32),
        compiler_params=_params("arbitrary"),
        name="outproj",
    )(x2d, oa, ob, oc, wa, wb, wc)


HEADS_PER_STEP = 8


def _moba_prompt_kernel(nb, slopes_ref, q_ref, kt_ref, vt_ref, g_ref, o_ref, kmt_ref, sel_ref, sd_ref, sdo_ref,
                        raw_ref):
    g = pl.program_id(1)
    i = pl.program_id(2)
    blk = MOBA_BLOCK
    nh = HEADS_PER_STEP
    width = nh * HEAD_DIM

    @pl.when(i == 0)
    def _():
        lane = lax.broadcasted_iota(jnp.int32, (width, LANES), 1)
        km = jnp.zeros((width, LANES), F32)
        for n in range(nb):
            col = jnp.sum(kt_ref[:, n * blk:(n + 1) * blk], axis=1, keepdims=True) * (1.0 / blk)
            km = jnp.where(lane == n, col, km)
        kmt_ref[...] = km
        tq = lax.broadcasted_iota(jnp.int32, (blk, blk), 1)
        tk = lax.broadcasted_iota(jnp.int32, (blk, blk), 0)
        d0 = (tq - tk).astype(F32)
        for h in range(nh):
            sd = (slopes_ref[nh * g + h] * LOG2E) * d0
            sd_ref[h] = sd
            sdo_ref[h] = jnp.where(d0 >= 0.0, sd, -NEG)

    qt = q_ref[...].T
    qts = _bf(qt * (QK_SCALE * LOG2E))

    def scores(h, start):
        r0 = h * HEAD_DIM
        kb = kt_ref[r0:r0 + HEAD_DIM, pl.ds(start, blk)]
        return _dot_tn(kb, qts[r0:r0 + HEAD_DIM])

    for h in range(nh):
        r0 = h * HEAD_DIM
        gt = _dot3_tn(kmt_ref[r0:r0 + HEAD_DIM, :], qt[r0:r0 + HEAD_DIM, :])[0:nb]
        sel_ref[h] = _topk_rows(gt, nb, i)

    def head_step(h, carry, start, sp, shift, keep=None):
        m_run, l_run, acc = carry
        r0 = h * HEAD_DIM
        m_blk = jnp.max(sp, axis=0, keepdims=True) - shift
        off = shift
        if keep is not None:
            m_blk = jnp.where(keep > 0.5, m_blk, NEG)
            off = jnp.where(keep > 0.5, shift, -4.0 * NEG)
        m_new = jnp.maximum(m_run, m_blk)
        alpha = jnp.exp2(m_run - m_new)
        pt = jnp.exp2(sp - (m_new + off))
        l_new = alpha * l_run + jnp.sum(pt, axis=0, keepdims=True)
        vb = vt_ref[r0:r0 + HEAD_DIM, pl.ds(start, blk)]
        acc_new = alpha * acc + _dot(vb, pt)
        return m_new, l_new, acc_new

    def past(n, carries):
        start = pl.multiple_of(n * blk, blk)
        nxt = pl.multiple_of((n + 1) * blk, blk)
        gap = ((i - n) * blk).astype(F32) * LOG2E
        out = []
        for h in range(nh):
            sp = raw_ref[h] - sd_ref[h]
            raw_ref[h] = scores(h, nxt)
            out.append(head_step(h, carries[h], start, sp, slopes_ref[nh * g + h] * gap,
                                 keep=sel_ref[h, pl.ds(n, 1), :]))
        return tuple(out)

    init = tuple((jnp.full((1, blk), NEG, F32), jnp.zeros((1, blk), F32), jnp.zeros((HEAD_DIM, blk), F32))
                 for _ in range(nh))
    for h in range(nh):
        raw_ref[h] = scores(h, 0)
    carries = lax.fori_loop(0, i, past, init)
    start = pl.multiple_of(i * blk, blk)
    outs = []
    for h in range(nh):
        _, l_fin, acc = head_step(h, carries[h], start, raw_ref[h] - sdo_ref[h], 0.0)
        outs.append(acc / l_fin)
    o_ref[...] = jnp.concatenate(outs, axis=0).T * g_ref[...]


def _moba_prompt(q, kt, vt, gate, slopes, layer):
    b, t, _ = q.shape
    nb = t // MOBA_BLOCK
    width = HEADS_PER_STEP * HEAD_DIM
    qspec = pl.BlockSpec((None, MOBA_BLOCK, width), lambda bi, g, i: (bi, i, g))
    kspec = pl.BlockSpec((None, None, width, t), lambda bi, g, i: (bi, layer, g, 0))
    return pl.pallas_call(
        functools.partial(_moba_prompt_kernel, nb),
        grid=(b, N_HEADS_A // HEADS_PER_STEP, nb),
        in_specs=[pl.BlockSpec(memory_space=pltpu.SMEM), qspec, kspec, kspec, qspec],
        out_specs=qspec,
        out_shape=jax.ShapeDtypeStruct(q.shape, F32),
        scratch_shapes=[pltpu.VMEM((width, LANES), F32), pltpu.VMEM((HEADS_PER_STEP, nb, MOBA_BLOCK), F32),
                        pltpu.VMEM((HEADS_PER_STEP, MOBA_BLOCK, MOBA_BLOCK), F32),
                        pltpu.VMEM((HEADS_PER_STEP, MOBA_BLOCK, MOBA_BLOCK), F32),
                        pltpu.VMEM((HEADS_PER_STEP, MOBA_BLOCK, MOBA_BLOCK), F32)],
        compiler_params=_params("arbitrary", "arbitrary", "arbitrary"),
        name="moba_prompt",
    )(slopes, q, kt, vt, gate)


SCORE_ROWS = 64
SEQS_PER_STEP = 2


def _moba_sample_kernel(n_pages, page, t_new, pt_ref, q_ref, kn_ref, vn_ref, g_ref, rowc_ref, hm_ref, *rest):
    n_ops = SEQS_PER_STEP * n_pages
    kp_refs = rest[:n_ops]
    vp_refs = rest[n_ops:2 * n_ops]
    o_ref = rest[2 * n_ops]
    s_ref = rest[2 * n_ops + 1]
    del pt_ref
    stages = [_moba_sample_seq(n_pages, page, t_new, s, q_ref, kn_ref, vn_ref, g_ref, rowc_ref, hm_ref,
                               kp_refs[s * n_pages:(s + 1) * n_pages], vp_refs[s * n_pages:(s + 1) * n_pages],
                               o_ref, s_ref)
              for s in range(SEQS_PER_STEP)]
    while stages:
        for gen in list(stages):
            if next(gen, StopIteration) is StopIteration:
                stages.remove(gen)


def _moba_sample_seq(n_pages, page, t_new, slot, q_ref, kn_ref, vn_ref, g_ref, rowc_ref, hm_ref, kp_refs, vp_refs,
                     o_ref, s_ref):
    tok = slice(slot * t_new, (slot + 1) * t_new)
    s_ref = s_ref.at[slot]
    nh = N_HEADS_A
    rows = SCORE_ROWS
    pages_per_blk = MOBA_BLOCK // page
    nblk = n_pages // pages_per_blk
    hm = hm_ref[...]
    q = q_ref[tok, :]
    qrep = jnp.concatenate([q] * (rows // t_new), axis=0) * hm
    qsb = _bf(qrep * QK_SCALE)
    slope = rowc_ref[:, 0:1]
    qpos = rowc_ref[:, 1:2]
    lane_f = lax.broadcasted_iota(jnp.int32, (rows, page), 1).astype(F32)

    lane_g = lax.broadcasted_iota(jnp.int32, (rows, LANES), 1)
    gates = jnp.zeros((rows, LANES), F32)
    for j in range(n_pages):
        kp = kp_refs[j][...].reshape(nh * HEAD_DIM, page)
        st = jnp.dot(qsb, _bf(kp), preferred_element_type=F32)
        gates = gates + jnp.where(lane_g == j // pages_per_blk, jnp.sum(st, axis=1, keepdims=True), 0.0)
        dist = (qpos - float(j * page)) - lane_f
        s_ref[:, j * page:(j + 1) * page] = st - slope * dist
    yield

    cnt = jnp.zeros((rows, LANES), jnp.int32)
    for m in range(nblk):
        col = gates[:, m:m + 1]
        cnt = cnt + ((col > gates) | ((col == gates) & (m < lane_g))).astype(jnp.int32)
    sel_t = (cnt < MOBA_TOPK).astype(F32)

    s_own = _dot_nt(qsb, kn_ref[tok, :])
    trow = lax.broadcasted_iota(jnp.int32, (rows, t_new), 0) % t_new
    tcol = lax.broadcasted_iota(jnp.int32, (rows, t_new), 1)
    dist_own = (trow - tcol).astype(F32)
    s_own = jnp.where(dist_own >= 0.0, s_own - slope * dist_own, NEG)
    yield

    mvec = jnp.full((rows, page), NEG, F32)
    for j in range(n_pages):
        n = j // pages_per_blk
        st = jnp.where(sel_t[:, n:n + 1] > 0.5, s_ref[:, j * page:(j + 1) * page], NEG)
        s_ref[:, j * page:(j + 1) * page] = st
        mvec = jnp.maximum(mvec, st)
    m_row = jnp.maximum(jnp.max(mvec, axis=1, keepdims=True), jnp.max(s_own, axis=1, keepdims=True))
    yield

    p_own = jnp.exp(s_own - m_row)
    lvec = jnp.zeros((rows, page), F32)
    acc = _dot(p_own, vn_ref[tok, :])
    for j in range(n_pages):
        pj = jnp.exp(s_ref[:, j * page:(j + 1) * page] - m_row)
        lvec = lvec + pj
        acc = acc + _dot_nt(pj, vp_refs[j][...].reshape(nh * HEAD_DIM, page))
    yield
    l_row = jnp.sum(lvec, axis=1, keepdims=True) + jnp.sum(p_own, axis=1, keepdims=True)
    tot = acc * (1.0 / l_row) * hm
    out = tot[0:t_new]
    for h in range(1, nh):
        out = out + tot[h * t_new:(h + 1) * t_new]
    o_ref[tok, :] = out * g_ref[tok, :]


def _moba_sample(q, k_new, v_new, gate, cache_kt, cache_vt, pt_flat, layer, consts, n_seq, t_new, n_pages):
    page = cache_kt.shape[-1]
    rowc, hm = consts
    per = SEQS_PER_STEP
    tspec = pl.BlockSpec((per * t_new, D_A), lambda b, pt: (b, 0))

    def page_spec(s, j):
        return pl.BlockSpec((None, None, N_HEADS_A, HEAD_DIM, page),
                            lambda b, pt, s=s, j=j: (pt[(b * per + s) * n_pages + j], layer, 0, 0, 0))

    cspec = lambda a: pl.BlockSpec(a.shape, lambda b, pt: (0,) * a.ndim)
    pages = [page_spec(s, j) for s in range(per) for j in range(n_pages)]
    grid_spec = pltpu.PrefetchScalarGridSpec(
        num_scalar_prefetch=1,
        grid=(n_seq // per,),
        in_specs=[tspec, tspec, tspec, tspec, cspec(rowc), cspec(hm)] + pages + pages,
        out_specs=tspec,
        scratch_shapes=[pltpu.VMEM((per, SCORE_ROWS, n_pages * page), F32)],
    )
    return pl.pallas_call(
        functools.partial(_moba_sample_kernel, n_pages, page, t_new),
        grid_spec=grid_spec,
        out_shape=jax.ShapeDtypeStruct(q.shape, F32),
        compiler_params=_params("arbitrary"),
        name="moba_sample",
    )(pt_flat, q, k_new, v_new, gate, rowc, hm, *([cache_kt] * (per * n_pages)), *([cache_vt] * (per * n_pages)))


def _moba_sample_consts(t_new, past_len):
    nh = N_HEADS_A
    used = nh * t_new
    r = np.arange(SCORE_ROWS)
    live = r < used
    slopes = 2.0 ** (-8.0 * (np.arange(nh) + 1) / nh)
    rowc = np.zeros((SCORE_ROWS, 2), np.float32)
    rowc[:, 0] = np.where(live, slopes[np.minimum(r // t_new, nh - 1)], 0.0)
    rowc[:, 1] = past_len + (r % t_new)
    c = np.arange(nh * HEAD_DIM)
    hm = (((c[None, :] // HEAD_DIM) == (r[:, None] // t_new)) & live[:, None]).astype(np.float32)
    return jnp.asarray(rowc), jnp.asarray(hm)


def _hgrn_consts(c):
    levels = int(round(math.log2(c)))
    t = np.arange(c)[:, None]
    u = np.arange(c)[None, :]
    mats = [u <= t]
    masks = [t == u]
    for j in range(1, levels + 1):
        p = 2 ** j
        hlf = p // 2
        mid = (t // p) * p + hlf
        upper = (t % p) >= hlf
        mats.append((upper & (u >= mid) & (u <= t)) | ((~upper) & (u > t) & (u <= mid - 1)))
        masks.append(((t // p) == (u // p)) & upper & ((u % p) < hlf))
    w_all = jnp.asarray(np.concatenate(mats, axis=0), dtype=BF16)
    mk = jnp.asarray(np.stack(masks), dtype=F32)
    return w_all, mk


def _hgrn_kernel(c, levels, q_ref, lf_ref, kk_ref, v_ref, g_ref, ng_ref, w_ref, mk_ref, e_ref,
                 o_ref, sn_ref, sbd_ref):
    nh = N_HEADS_B
    width = nh * HEAD_DIM
    lf = lf_ref[...]
    d_all = _dot_sel(w_ref[...], lf, parts=2)
    b = d_all[0:c]
    e_b = jnp.exp(b)
    e_end = jnp.exp(b[c - 1:c, :] - b)
    q = q_ref[...]
    k = kk_ref[...]
    v = v_ref[...]
    vb = _bf(v)
    sbd = sbd_ref[...]
    o = _dot(q * e_b, sbd)
    yield
    lane = lax.broadcasted_iota(jnp.int32, (c, width), 1)
    qk_levels = [(q, k)]
    for j in range(1, levels + 1):
        e_j = jnp.exp(d_all[j * c:(j + 1) * c])
        qk_levels.append((q * e_j, k * e_j))
    heads = [(lane >= h * HEAD_DIM) & (lane < (h + 1) * HEAD_DIM) for h in range(nh)]
    kbs = [_bf(kj) for _, kj in qk_levels]
    a_heads = []
    for h in range(nh):
        a = jnp.zeros((c, c), F32)
        for j, (qj, _) in enumerate(qk_levels):
            a = a + _dot_nt(jnp.where(heads[h], qj, 0.0), kbs[j]) * mk_ref[j]
        a_heads.append(_bf(a))
        yield
    for h in range(nh):
        o = o + jnp.where(heads[h], jnp.dot(a_heads[h], vb, preferred_element_type=F32), 0.0)

    yield
    ones = jnp.ones((c, LANES), BF16)
    hi, mid, lo = _split3(lf)
    dn = (((0,), (0,)), ((), ()))
    colsum = (lax.dot_general(hi, ones, dn, preferred_element_type=F32)
              + lax.dot_general(mid, ones, dn, preferred_element_type=F32)
              + lax.dot_general(lo, ones, dn, preferred_element_type=F32))
    decay = jnp.exp(colsum)
    decay = jnp.concatenate([decay] * (width // LANES), axis=1)
    r = lax.broadcasted_iota(jnp.int32, (width, width), 0) // HEAD_DIM
    cc = lax.broadcasted_iota(jnp.int32, (width, width), 1) // HEAD_DIM
    s_new = sbd * decay + jnp.where(r == cc, _dot_tn(k * e_end, v), 0.0)
    sbd_ref[...] = s_new
    yield

    ss = _seg_sum(o * o, e_ref)
    o_ref[...] = o * lax.rsqrt(ss * (1.0 / HEAD_DIM) + EPS) * ng_ref[...] * g_ref[...]
    for h in range(nh):
        sn_ref[h] = s_new[h * HEAD_DIM:(h + 1) * HEAD_DIM, h * HEAD_DIM:(h + 1) * HEAD_DIM]


def _mlstm_consts(c):
    t = np.arange(c)[:, None]
    u = np.arange(c)[None, :]
    return jnp.asarray(u <= t, dtype=BF16)


def _mlstm_kernel(c, uc_ref, g_ref, cw_ref, cb_ref, wq_ref, wk_ref, wv_ref,
                  wg_ref, bg_ref, skip_ref, ng_ref, tri_ref, e_ref,
                  o_ref, cn_ref, nn_ref, mn_ref, cvn_ref,
                  uext_ref, cbd_ref, n_ref, m_ref):
    nh = N_HEADS_C
    width = nh * HEAD_DIM

    uc = uc_ref[...]
    uext_ref[8:8 + c, :] = uc
    conv = cb_ref[...]
    for j in range(CONV_W):
        conv = conv + cw_ref[j:j + 1, :] * uext_ref[5 + j:5 + j + c, :]
    tail = uext_ref[c:c + 8, :]
    uext_ref[0:8, :] = tail
    uconv = _silu(conv)

    qm = _dot(uconv, wq_ref[...])
    km = _dot(uconv, wk_ref[...])
    vm = _dot(uc, wv_ref[...])
    i_raw = _dot3(qm, wg_ref[0]) + _dot3(km, wg_ref[1]) + _dot3(vm, wg_ref[2]) + bg_ref[...]
    logf = pltpu.roll(_log_sigmoid(i_raw), GATE_LANES - N_HEADS_C, 1)
    yield
    bcum = _dot_sel(tri_ref[...], logf)
    a = i_raw - bcum
    rowi = lax.broadcasted_iota(jnp.int32, (c, GATE_LANES), 0)
    s = 1
    while s < c:
        a = jnp.maximum(a, jnp.where(rowi >= s, pltpu.roll(a, s, 0), -jnp.inf))
        s *= 2
    m0 = m_ref[...]
    m_t = bcum + jnp.maximum(m0, a)
    g_in = jnp.exp(bcum + m0 - m_t)
    bm = bcum - m_t
    ib_t = (i_raw - bcum).T
    m_end = m_t[c - 1:c, :]
    b_end = bcum[c - 1:c, :]
    w_tok = jnp.exp((b_end - bcum) + i_raw - m_end)
    g_end = jnp.exp(b_end + m0 - m_end)

    yield
    ks = km * QK_SCALE
    ksb = _bf(ks)
    vmb = _bf(vm)
    lane = lax.broadcasted_iota(jnp.int32, (c, width), 1)
    trow = lax.broadcasted_iota(jnp.int32, (c, c), 0)
    tcol = lax.broadcasted_iota(jnp.int32, (c, c), 1)
    causal = tcol <= trow
    g256 = _spread_heads(g_in, c, nh)
    cbd = cbd_ref[...]
    n0 = n_ref[...]
    num = g256 * _dot(qm, cbd)
    qn = qm * n0
    den_cols = jnp.zeros((c, GATE_LANES), F32)
    lane_g = lax.broadcasted_iota(jnp.int32, (c, GATE_LANES), 1)
    heads = [(lane >= h * HEAD_DIM) & (lane < (h + 1) * HEAD_DIM) for h in range(nh)]
    raw = [_dot_nt(jnp.where(heads[h], qm, 0.0), ksb) for h in range(nh)]
    for h in range(nh):
        expo = bm[:, h:h + 1] + ib_t[h:h + 1, :]
        dmat = jnp.exp(jnp.where(causal, expo, NEG))
        qk = raw[h] * dmat
        num = num + jnp.where(heads[h], jnp.dot(_bf(qk), vmb, preferred_element_type=F32), 0.0)
        den_h = (g_in[:, h:h + 1] * jnp.sum(jnp.where(heads[h], qn, 0.0), axis=1, keepdims=True)
                 + jnp.sum(qk, axis=1, keepdims=True))
        den_cols = jnp.where(lane_g == h, den_h, den_cols)
        yield
    denom = jnp.maximum(jnp.abs(den_cols), jnp.exp(-m_t))
    hval = num / _spread_heads(denom, c, nh)

    w256 = _spread_heads(w_tok, c, nh)
    gend256 = _spread_heads(g_end, 1, nh)
    r = lax.broadcasted_iota(jnp.int32, (width, width), 0) // HEAD_DIM
    cc = lax.broadcasted_iota(jnp.int32, (width, width), 1) // HEAD_DIM
    kw = ks * w256
    c_new = cbd * gend256 + jnp.where(r == cc, _dot_tn(kw, vm), 0.0)
    n_new = gend256 * n0 + jnp.sum(kw, axis=0, keepdims=True)
    cbd_ref[...] = c_new
    n_ref[...] = n_new
    m_ref[...] = m_end
    yield

    mean = _seg_sum(hval, e_ref) * (1.0 / HEAD_DIM)
    xc = hval - mean
    var = _seg_sum(xc * xc, e_ref) * (1.0 / HEAD_DIM)
    hc = xc * lax.rsqrt(var + EPS) * ng_ref[...]
    o_ref[...] = (hc + skip_ref[...] * uconv) * g_ref[...]
    for h in range(nh):
        cn_ref[h] = c_new[h * HEAD_DIM:(h + 1) * HEAD_DIM, h * HEAD_DIM:(h + 1) * HEAD_DIM]
    nn_ref[...] = n_new
    mn_ref[...] = m_end
    cvn_ref[...] = tail


N_HGRN_IN, N_HGRN_OUT, N_HGRN_SCRATCH = 9, 2, 1
N_MLSTM_IN, N_MLSTM_OUT, N_MLSTM_SCRATCH = 13, 5, 4


def _recur_kernel(c, levels, *refs):
    i0 = 0
    hg_in = refs[i0:i0 + N_HGRN_IN]
    i0 += N_HGRN_IN
    ml_in = refs[i0:i0 + N_MLSTM_IN]
    i0 += N_MLSTM_IN
    hg_out = refs[i0:i0 + N_HGRN_OUT]
    i0 += N_HGRN_OUT
    ml_out = refs[i0:i0 + N_MLSTM_OUT]
    i0 += N_MLSTM_OUT
    hg_scr = refs[i0:i0 + N_HGRN_SCRATCH]
    i0 += N_HGRN_SCRATCH
    ml_scr = refs[i0:i0 + N_MLSTM_SCRATCH]

    @pl.when(pl.program_id(1) == 0)
    def _():
        for ref in hg_scr + ml_scr:
            ref[...] = jnp.zeros(ref.shape, F32)

    stages = [_hgrn_kernel(c, levels, *hg_in, *hg_out, *hg_scr), _mlstm_kernel(c, *ml_in, *ml_out, *ml_scr)]
    while stages:
        for gen in list(stages):
            if next(gen, StopIteration) is StopIteration:
                stages.remove(gen)


def _recur_prompt(qb, lf, kk, ib, gb, uc, gc, lw, shared, c):
    b, t, width = qb.shape
    w_all, mk = shared['hgrn_consts']
    levels = mk.shape[0] - 1
    tok = pl.BlockSpec((None, c, width), lambda bi, ci: (bi, ci, 0))
    per_b = lambda shp: pl.BlockSpec((None,) + shp, lambda bi, ci: (bi,) + (0,) * len(shp))
    hg_w = [lw['hng'], w_all, mk, shared['e256']]
    ml_w = [lw['conv_w'], lw['conv_b'], lw['wq'], lw['wk'], lw['wv'], lw['wg'], lw['bg'],
            lw['skip'], lw['mng'], shared['mlstm_tri'], shared['e256']]
    assert 5 + len(hg_w) == N_HGRN_IN and 2 + len(ml_w) == N_MLSTM_IN
    state = (N_HEADS_B, HEAD_DIM, HEAD_DIM)
    out_shape = [jax.ShapeDtypeStruct(qb.shape, F32), jax.ShapeDtypeStruct((b,) + state, F32),
                 jax.ShapeDtypeStruct(uc.shape, F32), jax.ShapeDtypeStruct((b,) + state, F32),
                 jax.ShapeDtypeStruct((b, 1, width), F32), jax.ShapeDtypeStruct((b, 1, GATE_LANES), F32),
                 jax.ShapeDtypeStruct((b, SUBLANES, width), F32)]
    return pl.pallas_call(
        functools.partial(_recur_kernel, c, levels),
        grid=(b, t // c),
        in_specs=[tok] * 5 + [_const_spec(w.shape) for w in hg_w] + [tok] * 2 + [_const_spec(w.shape) for w in ml_w],
        out_specs=[tok, per_b(state), tok, per_b(state), per_b((1, width)), per_b((1, GATE_LANES)),
                   per_b((SUBLANES, width))],
        out_shape=out_shape,
        scratch_shapes=[pltpu.VMEM((width, width), F32),
                        pltpu.VMEM((c + SUBLANES, width), F32), pltpu.VMEM((width, width), F32),
                        pltpu.VMEM((1, width), F32), pltpu.VMEM((1, GATE_LANES), F32)],
        compiler_params=_params("arbitrary", "arbitrary"),
        name="recur_prompt",
    )(qb, lf, kk, ib, gb, *hg_w, uc, gc, *ml_w)


K_UNROLL = 8


def _hgrn_sample_kernel(t_new, q_ref, kk_ref, v_ref, g_ref, s0_ref, ng_ref, o_ref, sn_ref):
    sn_ref[...] = s0_ref[...]
    for t in range(t_new):
        vt = v_ref[t]

        def kbody(kb, o, t=t):
            for kk in range(K_UNROLL):
                k = kb * K_UNROLL + kk
                kt = kk_ref[t, pl.ds(k, 1), :]
                s_k = (1.0 - kt) * sn_ref[k] + kt * vt
                sn_ref[k] = s_k
                o = o + s_k * q_ref[t, pl.ds(k, 1), :]
            return o

        o = lax.fori_loop(0, HEAD_DIM // K_UNROLL, kbody, jnp.zeros(vt.shape, F32))
        ss = jnp.sum(o * o, axis=0, keepdims=True) * (1.0 / HEAD_DIM)
        o_ref[t] = o * lax.rsqrt(ss + EPS) * ng_ref[...] * g_ref[t]


def _hgrn_sample(qt, kkt, vt, gt, s0, ng_col):
    t_new, width, b = qt.shape
    tok = pl.BlockSpec((t_new, HEAD_DIM, b), lambda h: (0, h, 0))
    st = pl.BlockSpec((None, HEAD_DIM, HEAD_DIM, b), lambda h: (h, 0, 0, 0))
    return pl.pallas_call(
        functools.partial(_hgrn_sample_kernel, t_new),
        grid=(N_HEADS_B,),
        in_specs=[tok, tok, tok, tok, st, pl.BlockSpec((HEAD_DIM, 1), lambda h: (h, 0))],
        out_specs=[tok, st],
        out_shape=[jax.ShapeDtypeStruct(qt.shape, F32), jax.ShapeDtypeStruct(s0.shape, F32)],
        compiler_params=_params("arbitrary"),
        name="hgrn_sample",
    )(qt, kkt, vt, gt, s0, ng_col)


def _mlstm_front_kernel(t_new, uc_ref, cv0_ref, cw_ref, cb_ref, wq_ref, wk_ref, wv_ref, wi_ref, wf_ref,
                        bi_ref, bf_ref, uconv_ref, qm_ref, km_ref, vm_ref, i_ref, lf_ref):
    hist = [cv0_ref[j] for j in range(CONV_W - 1)] + [uc_ref[t] for t in range(t_new)]
    for t in range(t_new):
        conv = cb_ref[...]
        for j in range(CONV_W):
            conv = conv + cw_ref[j] * hist[t + j]
        uconv = _silu(conv)
        uconv_ref[t] = uconv
        qm = _dot(wq_ref[...], uconv)
        km = _dot(wk_ref[...], uconv)
        vm = _dot(wv_ref[...], hist[t + CONV_W - 1])
        qm_ref[t] = qm
        km_ref[t] = km
        vm_ref[t] = vm
        i_ref[t] = _dot3(wi_ref[0], qm) + _dot3(wi_ref[1], km) + _dot3(wi_ref[2], vm) + bi_ref[...]
        lf_ref[t] = _log_sigmoid(_dot3(wf_ref[0], qm) + _dot3(wf_ref[1], km) + _dot3(wf_ref[2], vm)
                                 + bf_ref[...])


def _mlstm_front(uct, cv0t, lw):
    t_new, width, b = uct.shape
    weights = [lw['conv_w_col'], lw['conv_b_col'], lw['wq_t'], lw['wk_t'], lw['wv_t'], lw['wi_t'], lw['wf_t'],
               lw['bi_col'], lw['bf_col']]
    big = jax.ShapeDtypeStruct(uct.shape, F32)
    small = jax.ShapeDtypeStruct((t_new, SUBLANES, b), F32)
    return pl.pallas_call(
        functools.partial(_mlstm_front_kernel, t_new),
        grid=(1,),
        in_specs=[_const_spec(uct.shape), _const_spec(cv0t.shape)] + [_const_spec(w.shape) for w in weights],
        out_specs=[_const_spec(uct.shape)] * 4 + [_const_spec(small.shape)] * 2,
        out_shape=[big] * 4 + [small] * 2,
        compiler_params=_params("arbitrary"),
        name="mlstm_front",
    )(uct, cv0t, *weights)


def _mlstm_sample_kernel(t_new, qm_ref, km_ref, vm_ref, i_ref, lf_ref, uconv_ref, g_ref, c0_ref, n0_ref, m0_ref,
                         skip_ref, ng_ref, o_ref, cn_ref, nn_ref, mn_ref):
    h = pl.program_id(0)
    cn_ref[...] = c0_ref[...]
    n = n0_ref[...]
    m = m0_ref[...]
    for t in range(t_new):
        i_t = i_ref[t, pl.ds(h, 1), :]
        lf_t = lf_ref[t, pl.ds(h, 1), :]
        m_new = jnp.maximum(lf_t + m, i_t)
        fp = jnp.exp(lf_t + m - m_new)
        ip = jnp.exp(i_t - m_new)
        vt = vm_ref[t]
        n = fp * n + ip * (km_ref[t] * QK_SCALE)

        def kbody(kb, num, t=t, fp=fp, ip=ip, vt=vt):
            for kk in range(K_UNROLL):
                k = kb * K_UNROLL + kk
                kt = km_ref[t, pl.ds(k, 1), :] * QK_SCALE
                c_k = fp * cn_ref[k] + (ip * kt) * vt
                cn_ref[k] = c_k
                num = num + c_k * qm_ref[t, pl.ds(k, 1), :]
            return num

        num = lax.fori_loop(0, HEAD_DIM // K_UNROLL, kbody, jnp.zeros(vt.shape, F32))
        den = jnp.sum(qm_ref[t] * n, axis=0, keepdims=True)
        hval = num / jnp.maximum(jnp.abs(den), jnp.exp(-m_new))
        m = m_new
        mean = jnp.sum(hval, axis=0, keepdims=True) * (1.0 / HEAD_DIM)
        xc = hval - mean
        var = jnp.sum(xc * xc, axis=0, keepdims=True) * (1.0 / HEAD_DIM)
        hc = xc * lax.rsqrt(var + EPS) * ng_ref[...]
        o_ref[t] = (hc + skip_ref[...] * uconv_ref[t]) * g_ref[t]
    nn_ref[...] = n
    mn_ref[...] = m


def _mlstm_sample(front, gt, c0, n0, m0, lw):
    uconv, qm, km, vm, i_raw, logf = front
    t_new, width, b = qm.shape
    tok = pl.BlockSpec((t_new, HEAD_DIM, b), lambda h: (0, h, 0))
    gates = _const_spec(i_raw.shape)
    st = pl.BlockSpec((None, HEAD_DIM, HEAD_DIM, b), lambda h: (h, 0, 0, 0))
    nst = pl.BlockSpec((None, HEAD_DIM, b), lambda h: (h, 0, 0))
    mst = pl.BlockSpec((None, 1, b), lambda h: (h, 0, 0))
    col = pl.BlockSpec((HEAD_DIM, 1), lambda h: (h, 0))
    return pl.pallas_call(
        functools.partial(_mlstm_sample_kernel, t_new),
        grid=(N_HEADS_C,),
        in_specs=[tok, tok, tok, gates, gates, tok, tok, st, nst, mst, col, col],
        out_specs=[tok, st, nst, mst],
        out_shape=[jax.ShapeDtypeStruct(qm.shape, F32), jax.ShapeDtypeStruct(c0.shape, F32),
                   jax.ShapeDtypeStruct(n0.shape, F32), jax.ShapeDtypeStruct(m0.shape, F32)],
        compiler_params=_params("arbitrary"),
        name="mlstm_sample",
    )(qm, km, vm, i_raw, logf, uconv, gt, c0, n0, m0, lw['skip_col'], lw['mng_col'])


def _gate_weights(w_i, w_f):
    w3 = jnp.concatenate([w_i.reshape(3, D_C, N_HEADS_C), w_f.reshape(3, D_C, N_HEADS_C)], axis=-1)
    return jnp.pad(w3, ((0, 0), (0, 0), (0, GATE_LANES - 2 * N_HEADS_C)))


def _gate_weights_t(w):
    w3 = jnp.swapaxes(w.reshape(3, D_C, N_HEADS_C), 1, 2)
    return jnp.pad(w3, ((0, 0), (0, SUBLANES - N_HEADS_C), (0, 0)))


def _pad_lanes(v):
    return jnp.pad(v, ((0, 0),) * (v.ndim - 1) + ((0, GATE_LANES - v.shape[-1]),))


def _pad_rows_col(v):
    return jnp.pad(v, (0, SUBLANES - v.shape[0]))[:, None]


def _layer_weights(l, norm_g, w_in, q_norm_g, k_norm_g, hgrn_norm_g, mlstm_conv_w, mlstm_conv_b, mlstm_wq,
                   mlstm_wk, mlstm_wv, mlstm_w_ig, mlstm_b_ig, mlstm_w_fg, mlstm_b_fg, mlstm_skip, mlstm_norm_g,
                   w_out):
    bd = lambda w: _block_diag(w).astype(BF16)
    bdt = lambda w: _block_diag(jnp.swapaxes(w, 1, 2)).astype(BF16)
    return {
        'norm_g': norm_g[l][None, :],
        'w_in': w_in[l].astype(BF16),
        'qg': jnp.tile(q_norm_g[l], N_HEADS_A)[None, :],
        'kg': jnp.tile(k_norm_g[l], N_HEADS_A)[None, :],
        'hng': hgrn_norm_g[l][None, :],
        'hng_col': hgrn_norm_g[l][:, None],
        'conv_w': mlstm_conv_w[l],
        'conv_b': mlstm_conv_b[l][None, :],
        'conv_w_col': mlstm_conv_w[l][:, :, None],
        'conv_b_col': mlstm_conv_b[l][:, None],
        'wq': bd(mlstm_wq[l]), 'wk': bd(mlstm_wk[l]), 'wv': bd(mlstm_wv[l]),
        'wq_t': bdt(mlstm_wq[l]), 'wk_t': bdt(mlstm_wk[l]), 'wv_t': bdt(mlstm_wv[l]),
        'wg': _gate_weights(mlstm_w_ig[l], mlstm_w_fg[l]),
        'wi_t': _gate_weights_t(mlstm_w_ig[l]), 'wf_t': _gate_weights_t(mlstm_w_fg[l]),
        'bg': _pad_lanes(jnp.concatenate([mlstm_b_ig[l], mlstm_b_fg[l]])[None, :]),
        'bi_col': _pad_rows_col(mlstm_b_ig[l]), 'bf_col': _pad_rows_col(mlstm_b_fg[l]),
        'skip': mlstm_skip[l][None, :], 'mng': mlstm_norm_g[l][None, :],
        'skip_col': mlstm_skip[l][:, None], 'mng_col': mlstm_norm_g[l][:, None],
        'wo_a': w_out[l][:D_A].astype(BF16),
        'wo_b': w_out[l][D_A:D_A + D_B].astype(BF16),
        'wo_c': w_out[l][D_A + D_B:].astype(BF16),
    }


def _prompt_layer(x, layer, lw, shared, chunk, tm, depth, kv_prev):
    b, t, d = x.shape
    x2d = x.reshape(b * t, d)
    (qa, kt, vt, ga, qb, lf, kk, ib, gb, uc, gc) = _inproj(x2d, layer, lw, shared, tm, seq_len=t, depth=depth,
                                                           kv_prev=kv_prev)
    r3 = lambda a: a.reshape(b, t, a.shape[-1])
    oa = _moba_prompt(r3(qa), kt, vt, r3(ga), shared['slopes'], layer)
    ob, s_new, oc, c_new, n_new, m_new, cv_new = _recur_prompt(r3(qb), r3(lf), r3(kk), r3(ib), r3(gb), r3(uc),
                                                               r3(gc), lw, shared, chunk)
    y = _outproj(x2d, oa.reshape(b * t, D_A), ob.reshape(b * t, D_B), oc.reshape(b * t, D_C), lw, tm)
    return (y.reshape(b, t, d), kt, vt, s_new, c_new, n_new.reshape(b, N_HEADS_C, HEAD_DIM),
            m_new[:, 0, :N_HEADS_C], cv_new[:, SUBLANES - (CONV_W - 1):, :])


def _sample_layer(x, layer, lw, shared, cache_kt, cache_vt, pt_flat, s0, c0, n0, m0, cv0t, tm):
    b, t, d = x.shape
    n_pages = pt_flat.shape[0] // b
    x2d = x.reshape(b * t, d)
    (qa, ka, va, ga, qb, lf, kk, ib, gb, uc, gc) = _inproj(x2d, layer, lw, shared, tm)
    del lf
    oa = _moba_sample(qa, ka, va, ga, cache_kt, cache_vt, pt_flat, layer, shared['sample_consts'], b, t, n_pages)
    to_lanes = lambda a: jnp.transpose(a.reshape(b, t, a.shape[-1]), (1, 2, 0))
    from_lanes = lambda a: jnp.transpose(a, (2, 0, 1)).reshape(b * t, a.shape[1])
    obt, s_new = _hgrn_sample(to_lanes(qb), to_lanes(kk), to_lanes(ib), to_lanes(gb), s0, lw['hng_col'])
    front = _mlstm_front(to_lanes(uc), cv0t, lw)
    oct, c_new, n_new, m_new = _mlstm_sample(front, to_lanes(gc), c0, n0, m0, lw)
    y = _outproj(x2d, oa, from_lanes(obt), from_lanes(oct), lw, tm)
    conv_new = uc.reshape(b, t, D_C)[:, t - (CONV_W - 1):, :]
    return (y.reshape(b, t, d), ka.reshape(b, t, N_HEADS_A, HEAD_DIM), va.reshape(b, t, N_HEADS_A, HEAD_DIM),
            s_new, c_new, n_new, m_new, conv_new)


def kernel(x_prompt, x_sample, cache_k, cache_v, page_table, state_hgrn, state_mlstm_c, state_mlstm_n,
           state_mlstm_m, state_mlstm_conv, norm_g, w_in, q_norm_g, k_norm_g, hgrn_lb, hgrn_norm_g,
           mlstm_conv_w, mlstm_conv_b, mlstm_wq, mlstm_wk, mlstm_wv, mlstm_w_ig, mlstm_b_ig, mlstm_w_fg,
           mlstm_b_fg, mlstm_skip, mlstm_norm_g, w_out):
    depth = w_in.shape[0]
    bp, tp, _ = x_prompt.shape
    bd, td, _ = x_sample.shape
    n_pages = page_table.shape[1]
    page = cache_k.shape[2]
    chunk = min(tp, 256)
    shared = {
        'e256': _head_block_ones(),
        'hgrn_lb': hgrn_lb.astype(F32),
        'hgrn_consts': _hgrn_consts(chunk),
        'mlstm_tri': _mlstm_consts(chunk),
        'slopes': jnp.asarray(2.0 ** (-8.0 * (np.arange(N_HEADS_A) + 1) / N_HEADS_A), dtype=F32),
        'sample_consts': _moba_sample_consts(td, n_pages * page),
    }
    pt_flat = page_table.reshape(-1).astype(jnp.int32)
    cache_kt = jnp.transpose(cache_k, (0, 1, 3, 4, 2))
    cache_vt = jnp.transpose(cache_v, (0, 1, 3, 4, 2))
    s_h = jnp.transpose(state_hgrn, (0, 2, 3, 4, 1))
    s_c = jnp.transpose(state_mlstm_c, (0, 2, 3, 4, 1))
    s_n = jnp.transpose(state_mlstm_n, (0, 2, 3, 1))
    s_m = jnp.transpose(state_mlstm_m, (0, 2, 1))[:, :, None, :]
    s_cv = jnp.transpose(state_mlstm_conv, (0, 2, 3, 1))

    yp, ys = x_prompt, x_sample
    kv_prompt_buf = None
    outs_p = [[] for _ in range(7)]
    outs_s = [[] for _ in range(7)]
    for l in range(depth):
        lw = _layer_weights(l, norm_g, w_in, q_norm_g, k_norm_g, hgrn_norm_g, mlstm_conv_w, mlstm_conv_b,
                            mlstm_wq, mlstm_wk, mlstm_wv, mlstm_w_ig, mlstm_b_ig, mlstm_w_fg, mlstm_b_fg,
                            mlstm_skip, mlstm_norm_g, w_out)
        res_p = _prompt_layer(yp, l, lw, shared, chunk, min(PROMPT_ROW_TILE, tp), depth, kv_prompt_buf)
        res_s = _sample_layer(ys, l, lw, shared, cache_kt, cache_vt, pt_flat, s_h[l], s_c[l], s_n[l], s_m[l],
                              s_cv[l], SAMPLE_ROW_TILE)
        yp, ys = res_p[0], res_s[0]
        kv_prompt_buf = (res_p[1], res_p[2])
        for acc, a in zip(outs_p[2:], res_p[3:]):
            acc.append(a)
        for acc, a in zip(outs_s, res_s[1:]):
            acc.append(a)

    st = lambda lst, ax: jnp.stack(lst, axis=ax)

    def kv_prompt(buf):
        a = buf.reshape(bp, depth, N_HEADS_A, HEAD_DIM, tp)
        return jnp.transpose(a, (0, 1, 4, 2, 3))

    batch_first = lambda a: jnp.moveaxis(a, -1, 1)
    return (yp, ys, kv_prompt(kv_prompt_buf[0]), kv_prompt(kv_prompt_buf[1]), st(outs_s[0], 1), st(outs_s[1], 1),
            st(outs_p[2], 0), batch_first(st(outs_s[2], 0)), st(outs_p[3], 0), batch_first(st(outs_s[3], 0)),
            st(outs_p[4], 0), batch_first(st(outs_s[4], 0)), st(outs_p[5], 0),
            batch_first(st(outs_s[5], 0)[:, :, 0, :]), st(outs_p[6], 0), st(outs_s[6], 0))
```

```python
import functools
import math

import numpy as np
import jax
import jax.numpy as jnp
from jax import lax
from jax.experimental import pallas as pl
from jax.experimental.pallas import tpu as pltpu

F32 = jnp.float32
BF16 = jnp.bfloat16

HEAD_DIM = 64
N_HEADS_A = 8
N_HEADS_B = 4
N_HEADS_C = 4
D_A = N_HEADS_A * HEAD_DIM
D_B = N_HEADS_B * HEAD_DIM
D_C = N_HEADS_C * HEAD_DIM
MOBA_BLOCK = 256
MOBA_TOPK = 3
CONV_W = 4
EPS = 1e-6
NEG = -1e30
GATE_LANES = 128
LANES = 128
SUBLANES = 8
VMEM_LIMIT = 56 * 1024 * 1024
PROMPT_ROW_TILE = 512
SAMPLE_ROW_TILE = 256
QK_SCALE = HEAD_DIM ** -0.5
LOG2E = math.log2(math.e)


def _bf(x):
    return x.astype(BF16)


def _dot(a, b):
    return jnp.dot(_bf(a), _bf(b), preferred_element_type=F32)


def _dot_nt(a, b):
    return lax.dot_general(_bf(a), _bf(b), (((1,), (1,)), ((), ())), preferred_element_type=F32)


def _dot_tn(a, b):
    return lax.dot_general(_bf(a), _bf(b), (((0,), (0,)), ((), ())), preferred_element_type=F32)


def _split2(x):
    hi = _bf(x)
    lo = _bf(x - hi.astype(F32))
    return hi, lo


def _split3(x):
    hi = _bf(x)
    r = x - hi.astype(F32)
    mid = _bf(r)
    lo = _bf(r - mid.astype(F32))
    return hi, mid, lo


def _dot_sel(w01, x, parts=3):
    pieces = _split3(x)[:parts]
    out = jnp.dot(w01, pieces[0], preferred_element_type=F32)
    for p in pieces[1:]:
        out = out + jnp.dot(w01, p, preferred_element_type=F32)
    return out


def _dot3(a, b):
    ah, al = _split2(a)
    bh, bl = _split2(b)
    return (jnp.dot(ah, bh, preferred_element_type=F32) + jnp.dot(al, bh, preferred_element_type=F32)
            + jnp.dot(ah, bl, preferred_element_type=F32))


def _dot3_tn(a, b):
    ah, al = _split2(a)
    bh, bl = _split2(b)
    dn = (((0,), (0,)), ((), ()))
    return (lax.dot_general(ah, bh, dn, preferred_element_type=F32)
            + lax.dot_general(al, bh, dn, preferred_element_type=F32)
            + lax.dot_general(ah, bl, dn, preferred_element_type=F32))


def _seg_sum(x, e_ref):
    hi, lo = _split2(x)
    e = e_ref[...]
    return jnp.dot(hi, e, preferred_element_type=F32) + jnp.dot(lo, e, preferred_element_type=F32)


def _spread_heads(x, rows, n_heads):
    lane = lax.broadcasted_iota(jnp.int32, (rows, n_heads * HEAD_DIM), 1)
    out = jnp.zeros((rows, n_heads * HEAD_DIM), F32)
    for h in range(n_heads):
        in_head = (lane >= h * HEAD_DIM) & (lane < (h + 1) * HEAD_DIM)
        out = jnp.where(in_head, x[:, h:h + 1], out)
    return out


def _silu(x):
    return x * jax.nn.sigmoid(x)


def _log_sigmoid(x):
    return jnp.minimum(x, 0.0) - jnp.log(1.0 + jnp.exp(-jnp.abs(x)))


def _topk_rows(g, n_rows, limit):
    rid = lax.broadcasted_iota(jnp.int32, g.shape, 0)
    g = jnp.where(rid < limit, g, -jnp.inf)
    cnt = jnp.zeros(g.shape, jnp.int32)
    for m in range(n_rows):
        row = g[m:m + 1, :]
        beats = (row > g) | ((row == g) & (m < rid))
        cnt = cnt + beats.astype(jnp.int32)
    return ((cnt < MOBA_TOPK) & (rid < limit)).astype(F32)


def _const_spec(shape):
    nd = len(shape)
    return pl.BlockSpec(shape, lambda *_: (0,) * nd)


def _head_block_ones():
    r = np.arange(256)
    return jnp.asarray((r[:, None] // HEAD_DIM) == (r[None, :] // HEAD_DIM), dtype=BF16)


def _block_diag(w):
    h = w.shape[0]
    eye = jnp.eye(h, dtype=w.dtype)
    return jnp.einsum('hde,hg->hdge', w, eye).reshape(h * HEAD_DIM, h * HEAD_DIM)


def _params(*sem):
    return pltpu.CompilerParams(dimension_semantics=sem, vmem_limit_bytes=VMEM_LIMIT)


def _inproj_kernel(layer, kv_transposed, x_ref, g_ref, w_ref, qg_ref, kg_ref, lb_ref, e_ref,
                   qa_ref, ka_ref, va_ref, ga_ref, qb_ref, lf_ref, kk_ref, ib_ref, gb_ref, uc_ref, gc_ref):
    x = x_ref[...]
    h = x * lax.rsqrt(jnp.mean(x * x, axis=-1, keepdims=True) + EPS) * g_ref[...]
    hb = _bf(h)

    def proj(c0, width):
        return jnp.dot(hb, w_ref[:, c0:c0 + width], preferred_element_type=F32)

    def head_rms(p, g):
        halves = [_seg_sum(p[:, c:c + 256] * p[:, c:c + 256], e_ref) for c in (0, 256)]
        ss = jnp.concatenate(halves, axis=1)
        return p * lax.rsqrt(ss * (1.0 / HEAD_DIM) + EPS) * g

    qa_ref[...] = head_rms(proj(0, D_A), qg_ref[...])
    ka = head_rms(proj(D_A, D_A), kg_ref[...])
    va = proj(2 * D_A, D_A)
    if kv_transposed:
        ka_ref[...] = ka.T
        va_ref[...] = va.T
    else:
        ka_ref[...] = ka
        va_ref[...] = va
    ga_ref[...] = _silu(proj(3 * D_A, D_A))
    c = 4 * D_A
    qb_ref[...] = proj(c, D_B)
    lbp = lb_ref[...]
    lbe = jnp.exp(lbp - jnp.max(lbp, axis=0, keepdims=True))
    lbw = lbe / jnp.sum(lbe, axis=0, keepdims=True)
    lb_cum = lbw[0:1, :]
    for j in range(1, layer + 1):
        lb_cum = lb_cum + lbw[j:j + 1, :]
    lb = lb_cum - lbw[0:1, :]
    fg = lb + (1.0 - lb) * jax.nn.sigmoid(proj(c + D_B, D_B))
    lf_ref[...] = jnp.log(fg)
    kk_ref[...] = 1.0 - fg
    ib_ref[...] = proj(c + 2 * D_B, D_B)
    gb_ref[...] = _silu(proj(c + 3 * D_B, D_B))
    c = c + 4 * D_B
    uc_ref[...] = proj(c, D_C)
    gc_ref[...] = _silu(proj(c + D_C, D_C))


N_INPROJ_IN = 7


def _inproj_kernel_stacked(layer, *refs):
    _inproj_kernel(layer, True, *refs[:N_INPROJ_IN], *refs[N_INPROJ_IN + 2:])


def _inproj(x2d, layer, lw, shared, tm, seq_len=None, depth=None, kv_prev=None):
    m, d = x2d.shape
    kv_transposed = seq_len is not None
    widths = [D_A] * 4 + [D_B] * 5 + [D_C] * 2
    out_shape = [jax.ShapeDtypeStruct((m, w), F32) for w in widths]
    out_specs = [pl.BlockSpec((tm, w), lambda i: (i, 0)) for w in widths]
    if kv_transposed:
        tiles = seq_len // tm
        for idx in (1, 2):
            out_shape[idx] = jax.ShapeDtypeStruct((m // seq_len, depth, D_A, seq_len), F32)
            out_specs[idx] = pl.BlockSpec((None, None, D_A, tm), lambda i: (i // tiles, layer, 0, i % tiles))
    in_specs = [pl.BlockSpec((tm, d), lambda i: (i, 0)),
                lw.spec('norm_g'), lw.spec('w_in'), lw.spec('qg'), lw.spec('kg'),
                _const_spec(shared['hgrn_lb'].shape), _const_spec((256, 256))]
    args = [x2d, lw['norm_g'], lw['w_in'], lw['qg'], lw['kg'], shared['hgrn_lb'], shared['e256']]
    assert len(args) == N_INPROJ_IN
    body = functools.partial(_inproj_kernel, layer, kv_transposed)
    aliases = {}
    if kv_prev is not None:
        in_specs += [pl.BlockSpec(memory_space=pl.ANY)] * 2
        args += list(kv_prev)
        aliases = {N_INPROJ_IN: 1, N_INPROJ_IN + 1: 2}
        body = functools.partial(_inproj_kernel_stacked, layer)
    return pl.pallas_call(
        body,
        grid=(m // tm,),
        in_specs=in_specs,
        out_specs=out_specs,
        out_shape=out_shape,
        input_output_aliases=aliases,
        compiler_params=_params("arbitrary"),
        name="inproj",
    )(*args)


def _outproj_kernel(x_ref, oa_ref, ob_ref, oc_ref, wa_ref, wb_ref, wc_ref, y_ref):
    y_ref[...] = (x_ref[...] + _dot(oa_ref[...], wa_ref[...]) + _dot(ob_ref[...], wb_ref[...])
                  + _dot(oc_ref[...], wc_ref[...]))


def _outproj(x2d, oa, ob, oc, lw, tm):
    m, d = x2d.shape
    row = lambda w: pl.BlockSpec((tm, w), lambda i: (i, 0))
    return pl.pallas_call(
        _outproj_kernel,
        grid=(m // tm,),
        in_specs=[row(d), row(D_A), row(D_B), row(D_C), lw.spec('wo_a'), lw.spec('wo_b'), lw.spec('wo_c')],
        out_specs=row(d),
        out_shape=jax.ShapeDtypeStruct((m, d), F32),
        compiler_params=_params("arbitrary"),
        name="outproj",
    )(x2d, oa, ob, oc, lw['wo_a'], lw['wo_b'], lw['wo_c'])


HEADS_PER_STEP = 8


def _moba_prompt_kernel(nb, slopes_ref, q_ref, kt_ref, vt_ref, g_ref, o_ref, kmt_ref, sel_ref, sd_ref, sdo_ref,
                        raw_ref):
    g = pl.program_id(1)
    i = pl.program_id(2)
    blk = MOBA_BLOCK
    nh = HEADS_PER_STEP
    width = nh * HEAD_DIM

    @pl.when(i == 0)
    def _():
        lane = lax.broadcasted_iota(jnp.int32, (width, LANES), 1)
        km = jnp.zeros((width, LANES), F32)
        for n in range(nb):
            col = jnp.sum(kt_ref[:, n * blk:(n + 1) * blk], axis=1, keepdims=True) * (1.0 / blk)
            km = jnp.where(lane == n, col, km)
        km_rows = jnp.concatenate([km.T[0:nb]] * nh, axis=0)
        r_head = lax.broadcasted_iota(jnp.int32, (nh * nb, width), 0) // nb
        c_head = lax.broadcasted_iota(jnp.int32, (nh * nb, width), 1) // HEAD_DIM
        km_hi, km_lo = _split2(jnp.where(r_head == c_head, km_rows, 0.0))
        kmt_ref[0] = km_hi.astype(BF16)
        kmt_ref[1] = km_lo.astype(BF16)
        tq = lax.broadcasted_iota(jnp.int32, (blk, blk), 1)
        tk = lax.broadcasted_iota(jnp.int32, (blk, blk), 0)
        d0 = (tq - tk).astype(F32)
        for h in range(nh):
            sd = (slopes_ref[nh * g + h] * LOG2E) * d0
            sd_ref[h] = sd
            sdo_ref[h] = jnp.where(d0 >= 0.0, sd, -NEG)

    qt = q_ref[...].T
    qts = _bf(qt * (QK_SCALE * LOG2E))

    def scores(h, start):
        r0 = h * HEAD_DIM
        kb = kt_ref[r0:r0 + HEAD_DIM, pl.ds(start, blk)]
        return _dot_tn(kb, qts[r0:r0 + HEAD_DIM])

    q_hi, q_lo = _split2(qt)
    km_hi = kmt_ref[0]
    gates = (jnp.dot(km_hi, q_hi, preferred_element_type=F32) + jnp.dot(kmt_ref[1], q_hi, preferred_element_type=F32)
             + jnp.dot(km_hi, q_lo, preferred_element_type=F32))
    for h in range(nh):
        sel_ref[h] = _topk_rows(gates[h * nb:(h + 1) * nb], nb, i)

    def head_step(h, carry, start, sp, shift, keep=None):
        m_run, l_run, acc = carry
        r0 = h * HEAD_DIM
        m_blk = jnp.max(sp, axis=0, keepdims=True) - shift
        off = shift
        if keep is not None:
            m_blk = jnp.where(keep > 0.5, m_blk, NEG)
            off = jnp.where(keep > 0.5, shift, -4.0 * NEG)
        m_new = jnp.maximum(m_run, m_blk)
        alpha = jnp.exp2(m_run - m_new)
        pt = jnp.exp2(sp - (m_new + off))
        l_new = alpha * l_run + jnp.sum(pt, axis=0, keepdims=True)
        vb = vt_ref[r0:r0 + HEAD_DIM, pl.ds(start, blk)]
        acc_new = alpha * acc + _dot(vb, pt)
        return m_new, l_new, acc_new

    def past(n, carries):
        start = pl.multiple_of(n * blk, blk)
        nxt = pl.multiple_of((n + 1) * blk, blk)
        gap = ((i - n) * blk).astype(F32) * LOG2E
        out = []
        for h in range(nh):
            sp = raw_ref[h] - sd_ref[h]
            raw_ref[h] = scores(h, nxt)
            out.append(head_step(h, carries[h], start, sp, slopes_ref[nh * g + h] * gap,
                                 keep=sel_ref[h, pl.ds(n, 1), :]))
        return tuple(out)

    init = tuple((jnp.full((1, blk), NEG, F32), jnp.zeros((1, blk), F32), jnp.zeros((HEAD_DIM, blk), F32))
                 for _ in range(nh))
    for h in range(nh):
        raw_ref[h] = scores(h, 0)
    carries = lax.fori_loop(0, i, past, init)
    start = pl.multiple_of(i * blk, blk)
    outs = []
    for h in range(nh):
        _, l_fin, acc = head_step(h, carries[h], start, raw_ref[h] - sdo_ref[h], 0.0)
        outs.append(acc / l_fin)
    o_ref[...] = jnp.concatenate(outs, axis=0).T * g_ref[...]


def _moba_prompt(q, kt, vt, gate, slopes, layer):
    b, t, _ = q.shape
    nb = t // MOBA_BLOCK
    width = HEADS_PER_STEP * HEAD_DIM
    qspec = pl.BlockSpec((None, MOBA_BLOCK, width), lambda bi, g, i: (bi, i, g))
    kspec = pl.BlockSpec((None, None, width, t), lambda bi, g, i: (bi, layer, g, 0))
    return pl.pallas_call(
        functools.partial(_moba_prompt_kernel, nb),
        grid=(b, N_HEADS_A // HEADS_PER_STEP, nb),
        in_specs=[pl.BlockSpec(memory_space=pltpu.SMEM), qspec, kspec, kspec, qspec],
        out_specs=qspec,
        out_shape=jax.ShapeDtypeStruct(q.shape, F32),
        scratch_shapes=[pltpu.VMEM((2, HEADS_PER_STEP * nb, width), BF16),
                        pltpu.VMEM((HEADS_PER_STEP, nb, MOBA_BLOCK), F32),
                        pltpu.VMEM((HEADS_PER_STEP, MOBA_BLOCK, MOBA_BLOCK), F32),
                        pltpu.VMEM((HEADS_PER_STEP, MOBA_BLOCK, MOBA_BLOCK), F32),
                        pltpu.VMEM((HEADS_PER_STEP, MOBA_BLOCK, MOBA_BLOCK), F32)],
        compiler_params=_params("arbitrary", "arbitrary", "arbitrary"),
        name="moba_prompt",
    )(slopes, q, kt, vt, gate)


SCORE_ROWS = 64
SEQS_PER_STEP = 2


def _moba_sample_kernel(n_pages, page, t_new, pt_ref, q_ref, kn_ref, vn_ref, g_ref, rowc_ref, hm_ref, *rest):
    n_ops = SEQS_PER_STEP * n_pages
    kp_refs = rest[:n_ops]
    vp_refs = rest[n_ops:2 * n_ops]
    o_ref = rest[2 * n_ops]
    s_ref = rest[2 * n_ops + 1]
    del pt_ref
    stages = [_moba_sample_seq(n_pages, page, t_new, s, q_ref, kn_ref, vn_ref, g_ref, rowc_ref, hm_ref,
                               kp_refs[s * n_pages:(s + 1) * n_pages], vp_refs[s * n_pages:(s + 1) * n_pages],
                               o_ref, s_ref)
              for s in range(SEQS_PER_STEP)]
    while stages:
        for gen in list(stages):
            if next(gen, StopIteration) is StopIteration:
                stages.remove(gen)


def _moba_sample_seq(n_pages, page, t_new, slot, q_ref, kn_ref, vn_ref, g_ref, rowc_ref, hm_ref, kp_refs, vp_refs,
                     o_ref, s_ref):
    tok = slice(slot * t_new, (slot + 1) * t_new)
    s_ref = s_ref.at[slot]
    nh = N_HEADS_A
    rows = SCORE_ROWS
    pages_per_blk = MOBA_BLOCK // page
    nblk = n_pages // pages_per_blk
    hm = hm_ref[...]
    q = q_ref[tok, :]
    qrep = jnp.concatenate([q] * (rows // t_new), axis=0) * hm
    qsb = _bf(qrep * QK_SCALE)
    slope = rowc_ref[:, 0:1]
    qpos = rowc_ref[:, 1:2]
    lane_f = lax.broadcasted_iota(jnp.int32, (rows, page), 1).astype(F32)

    lane_g = lax.broadcasted_iota(jnp.int32, (rows, LANES), 1)
    gates = jnp.zeros((rows, LANES), F32)
    for j in range(n_pages):
        kp = kp_refs[j][...].reshape(nh * HEAD_DIM, page)
        st = jnp.dot(qsb, _bf(kp), preferred_element_type=F32)
        gates = gates + jnp.where(lane_g == j // pages_per_blk, jnp.sum(st, axis=1, keepdims=True), 0.0)
        dist = (qpos - float(j * page)) - lane_f
        s_ref[:, j * page:(j + 1) * page] = st - slope * dist
    yield

    cnt = jnp.zeros((rows, LANES), jnp.int32)
    for m in range(nblk):
        col = gates[:, m:m + 1]
        cnt = cnt + ((col > gates) | ((col == gates) & (m < lane_g))).astype(jnp.int32)
    sel_t = (cnt < MOBA_TOPK).astype(F32)

    s_own = _dot_nt(qsb, kn_ref[tok, :])
    trow = lax.broadcasted_iota(jnp.int32, (rows, t_new), 0) % t_new
    tcol = lax.broadcasted_iota(jnp.int32, (rows, t_new), 1)
    dist_own = (trow - tcol).astype(F32)
    s_own = jnp.where(dist_own >= 0.0, s_own - slope * dist_own, NEG)
    yield

    mvec = jnp.full((rows, page), NEG, F32)
    for j in range(n_pages):
        n = j // pages_per_blk
        st = jnp.where(sel_t[:, n:n + 1] > 0.5, s_ref[:, j * page:(j + 1) * page], NEG)
        s_ref[:, j * page:(j + 1) * page] = st
        mvec = jnp.maximum(mvec, st)
    m_row = jnp.maximum(jnp.max(mvec, axis=1, keepdims=True), jnp.max(s_own, axis=1, keepdims=True))
    yield

    p_own = jnp.exp(s_own - m_row)
    lvec = jnp.zeros((rows, page), F32)
    acc = _dot(p_own, vn_ref[tok, :])
    for j in range(n_pages):
        pj = jnp.exp(s_ref[:, j * page:(j + 1) * page] - m_row)
        lvec = lvec + pj
        acc = acc + _dot_nt(pj, vp_refs[j][...].reshape(nh * HEAD_DIM, page))
    yield
    l_row = jnp.sum(lvec, axis=1, keepdims=True) + jnp.sum(p_own, axis=1, keepdims=True)
    tot = acc * (1.0 / l_row) * hm
    out = tot[0:t_new]
    for h in range(1, nh):
        out = out + tot[h * t_new:(h + 1) * t_new]
    o_ref[tok, :] = out * g_ref[tok, :]


def _moba_sample(q, k_new, v_new, gate, cache_kt, cache_vt, pt_flat, layer, consts, n_seq, t_new, n_pages):
    page = cache_kt.shape[-1]
    rowc, hm = consts
    per = SEQS_PER_STEP
    tspec = pl.BlockSpec((per * t_new, D_A), lambda b, pt: (b, 0))

    def page_spec(s, j):
        return pl.BlockSpec((None, None, N_HEADS_A, HEAD_DIM, page),
                            lambda b, pt, s=s, j=j: (pt[(b * per + s) * n_pages + j], layer, 0, 0, 0))

    cspec = lambda a: pl.BlockSpec(a.shape, lambda b, pt: (0,) * a.ndim)
    pages = [page_spec(s, j) for s in range(per) for j in range(n_pages)]
    grid_spec = pltpu.PrefetchScalarGridSpec(
        num_scalar_prefetch=1,
        grid=(n_seq // per,),
        in_specs=[tspec, tspec, tspec, tspec, cspec(rowc), cspec(hm)] + pages + pages,
        out_specs=tspec,
        scratch_shapes=[pltpu.VMEM((per, SCORE_ROWS, n_pages * page), F32)],
    )
    return pl.pallas_call(
        functools.partial(_moba_sample_kernel, n_pages, page, t_new),
        grid_spec=grid_spec,
        out_shape=jax.ShapeDtypeStruct(q.shape, F32),
        compiler_params=_params("arbitrary"),
        name="moba_sample",
    )(pt_flat, q, k_new, v_new, gate, rowc, hm, *([cache_kt] * (per * n_pages)), *([cache_vt] * (per * n_pages)))


def _moba_sample_consts(t_new, past_len):
    nh = N_HEADS_A
    used = nh * t_new
    r = np.arange(SCORE_ROWS)
    live = r < used
    slopes = 2.0 ** (-8.0 * (np.arange(nh) + 1) / nh)
    rowc = np.zeros((SCORE_ROWS, 2), np.float32)
    rowc[:, 0] = np.where(live, slopes[np.minimum(r // t_new, nh - 1)], 0.0)
    rowc[:, 1] = past_len + (r % t_new)
    c = np.arange(nh * HEAD_DIM)
    hm = (((c[None, :] // HEAD_DIM) == (r[:, None] // t_new)) & live[:, None]).astype(np.float32)
    return jnp.asarray(rowc), jnp.asarray(hm)


def _hgrn_consts(c):
    levels = int(round(math.log2(c)))
    t = np.arange(c)[:, None]
    u = np.arange(c)[None, :]
    mats = [u <= t]
    masks = [t == u]
    for j in range(1, levels + 1):
        p = 2 ** j
        hlf = p // 2
        mid = (t // p) * p + hlf
        upper = (t % p) >= hlf
        mats.append((upper & (u >= mid) & (u <= t)) | ((~upper) & (u > t) & (u <= mid - 1)))
        masks.append(((t // p) == (u // p)) & upper & ((u % p) < hlf))
    w_all = jnp.asarray(np.concatenate(mats, axis=0), dtype=BF16)
    mk = jnp.asarray(np.stack(masks), dtype=F32)
    return w_all, mk


def _hgrn_kernel(c, levels, q_ref, lf_ref, kk_ref, v_ref, g_ref, ng_ref, w_ref, mk_ref, e_ref,
                 o_ref, sn_ref, sbd_ref):
    nh = N_HEADS_B
    width = nh * HEAD_DIM
    lf = lf_ref[...]
    d_all = _dot_sel(w_ref[...], lf, parts=2)
    b = d_all[0:c]
    e_b = jnp.exp(b)
    e_end = jnp.exp(b[c - 1:c, :] - b)
    q = q_ref[...]
    k = kk_ref[...]
    v = v_ref[...]
    vb = _bf(v)
    sbd = sbd_ref[...]
    o = _dot(q * e_b, sbd)
    yield
    lane = lax.broadcasted_iota(jnp.int32, (c, width), 1)
    qk_levels = [(q, k)]
    for j in range(1, levels + 1):
        e_j = jnp.exp(d_all[j * c:(j + 1) * c])
        qk_levels.append((q * e_j, k * e_j))
    heads = [(lane >= h * HEAD_DIM) & (lane < (h + 1) * HEAD_DIM) for h in range(nh)]
    kbs = [_bf(kj) for _, kj in qk_levels]
    a_heads = []
    for h in range(nh):
        a = jnp.zeros((c, c), F32)
        for j, (qj, _) in enumerate(qk_levels):
            a = a + _dot_nt(jnp.where(heads[h], qj, 0.0), kbs[j]) * mk_ref[j]
        a_heads.append(_bf(a))
        yield
    for h in range(nh):
        o = o + jnp.where(heads[h], jnp.dot(a_heads[h], vb, preferred_element_type=F32), 0.0)

    yield
    b_end_col = jnp.broadcast_to(b[c - 1:c, :], (SUBLANES, width)).T[:, 0:1]
    decay = jnp.exp(b_end_col)
    r = lax.broadcasted_iota(jnp.int32, (width, width), 0) // HEAD_DIM
    cc = lax.broadcasted_iota(jnp.int32, (width, width), 1) // HEAD_DIM
    s_new = sbd * decay + jnp.where(r == cc, _dot_tn(k * e_end, v), 0.0)
    sbd_ref[...] = s_new
    yield

    ss = _seg_sum(o * o, e_ref)
    o_ref[...] = o * lax.rsqrt(ss * (1.0 / HEAD_DIM) + EPS) * ng_ref[...] * g_ref[...]
    for h in range(nh):
        sn_ref[h] = s_new[h * HEAD_DIM:(h + 1) * HEAD_DIM, h * HEAD_DIM:(h + 1) * HEAD_DIM]


def _mlstm_consts(c):
    t = np.arange(c)[:, None]
    u = np.arange(c)[None, :]
    return jnp.asarray(u <= t, dtype=BF16)


def _mlstm_kernel(c, uc_ref, g_ref, cw_ref, cb_ref, wq_ref, wk_ref, wv_ref,
                  wg_ref, bg_ref, skip_ref, ng_ref, tri_ref, e_ref,
                  o_ref, cn_ref, nn_ref, mn_ref, cvn_ref,
                  uext_ref, cbd_ref, n_ref, m_ref):
    nh = N_HEADS_C
    width = nh * HEAD_DIM

    uc = uc_ref[...]
    uext_ref[8:8 + c, :] = uc
    conv = cb_ref[...]
    for j in range(CONV_W):
        conv = conv + cw_ref[j:j + 1, :] * uext_ref[5 + j:5 + j + c, :]
    tail = uext_ref[c:c + 8, :]
    uext_ref[0:8, :] = tail
    uconv = _silu(conv)

    qm = _dot(uconv, wq_ref[...])
    km = _dot(uconv, wk_ref[...])
    vm = _dot(uc, wv_ref[...])
    i_raw = _dot3(qm, wg_ref[0]) + _dot3(km, wg_ref[1]) + _dot3(vm, wg_ref[2]) + bg_ref[...]
    logf = pltpu.roll(_log_sigmoid(i_raw), GATE_LANES - N_HEADS_C, 1)
    yield
    bcum = _dot_sel(tri_ref[...], logf)
    a = i_raw - bcum
    rowi = lax.broadcasted_iota(jnp.int32, (c, GATE_LANES), 0)
    s = 1
    while s < c:
        a = jnp.maximum(a, jnp.where(rowi >= s, pltpu.roll(a, s, 0), -jnp.inf))
        s *= 2
    m0 = m_ref[...]
    m_t = bcum + jnp.maximum(m0, a)
    g_in = jnp.exp(bcum + m0 - m_t)
    bm = bcum - m_t
    ib_t = (i_raw - bcum).T
    m_end = m_t[c - 1:c, :]
    b_end = bcum[c - 1:c, :]
    w_tok = jnp.exp((b_end - bcum) + i_raw - m_end)
    g_end = jnp.exp(b_end + m0 - m_end)

    yield
    ks = km * QK_SCALE
    ksb = _bf(ks)
    vmb = _bf(vm)
    lane = lax.broadcasted_iota(jnp.int32, (c, width), 1)
    trow = lax.broadcasted_iota(jnp.int32, (c, c), 0)
    tcol = lax.broadcasted_iota(jnp.int32, (c, c), 1)
    causal = tcol <= trow
    g256 = _spread_heads(g_in, c, nh)
    cbd = cbd_ref[...]
    n0 = n_ref[...]
    num = g256 * _dot(qm, cbd)
    qn = qm * n0
    den_cols = jnp.zeros((c, GATE_LANES), F32)
    lane_g = lax.broadcasted_iota(jnp.int32, (c, GATE_LANES), 1)
    heads = [(lane >= h * HEAD_DIM) & (lane < (h + 1) * HEAD_DIM) for h in range(nh)]
    raw = [_dot_nt(jnp.where(heads[h], qm, 0.0), ksb) for h in range(nh)]
    for h in range(nh):
        expo = bm[:, h:h + 1] + ib_t[h:h + 1, :]
        dmat = jnp.exp(jnp.where(causal, expo, NEG))
        qk = raw[h] * dmat
        num = num + jnp.where(heads[h], jnp.dot(_bf(qk), vmb, preferred_element_type=F32), 0.0)
        den_h = (g_in[:, h:h + 1] * jnp.sum(jnp.where(heads[h], qn, 0.0), axis=1, keepdims=True)
                 + jnp.sum(qk, axis=1, keepdims=True))
        den_cols = jnp.where(lane_g == h, den_h, den_cols)
        yield
    denom = jnp.maximum(jnp.abs(den_cols), jnp.exp(-m_t))
    hval = num / _spread_heads(denom, c, nh)

    w256 = _spread_heads(w_tok, c, nh)
    gend256 = _spread_heads(g_end, 1, nh)
    r = lax.broadcasted_iota(jnp.int32, (width, width), 0) // HEAD_DIM
    cc = lax.broadcasted_iota(jnp.int32, (width, width), 1) // HEAD_DIM
    kw = ks * w256
    c_new = cbd * gend256 + jnp.where(r == cc, _dot_tn(kw, vm), 0.0)
    n_new = gend256 * n0 + jnp.sum(kw, axis=0, keepdims=True)
    cbd_ref[...] = c_new
    n_ref[...] = n_new
    m_ref[...] = m_end
    yield

    mean = _seg_sum(hval, e_ref) * (1.0 / HEAD_DIM)
    xc = hval - mean
    var = _seg_sum(xc * xc, e_ref) * (1.0 / HEAD_DIM)
    hc = xc * lax.rsqrt(var + EPS) * ng_ref[...]
    o_ref[...] = (hc + skip_ref[...] * uconv) * g_ref[...]
    for h in range(nh):
        cn_ref[h] = c_new[h * HEAD_DIM:(h + 1) * HEAD_DIM, h * HEAD_DIM:(h + 1) * HEAD_DIM]
    nn_ref[...] = n_new
    mn_ref[...] = m_end
    cvn_ref[...] = tail


N_HGRN_IN, N_HGRN_OUT, N_HGRN_SCRATCH = 9, 2, 1
N_MLSTM_IN, N_MLSTM_OUT, N_MLSTM_SCRATCH = 13, 5, 4


def _recur_kernel(c, levels, *refs):
    i0 = 0
    hg_in = refs[i0:i0 + N_HGRN_IN]
    i0 += N_HGRN_IN
    ml_in = refs[i0:i0 + N_MLSTM_IN]
    i0 += N_MLSTM_IN
    hg_out = refs[i0:i0 + N_HGRN_OUT]
    i0 += N_HGRN_OUT
    ml_out = refs[i0:i0 + N_MLSTM_OUT]
    i0 += N_MLSTM_OUT
    hg_scr = refs[i0:i0 + N_HGRN_SCRATCH]
    i0 += N_HGRN_SCRATCH
    ml_scr = refs[i0:i0 + N_MLSTM_SCRATCH]

    @pl.when(pl.program_id(1) == 0)
    def _():
        for ref in hg_scr + ml_scr:
            ref[...] = jnp.zeros(ref.shape, F32)

    stages = [_hgrn_kernel(c, levels, *hg_in, *hg_out, *hg_scr), _mlstm_kernel(c, *ml_in, *ml_out, *ml_scr)]
    while stages:
        for gen in list(stages):
            if next(gen, StopIteration) is StopIteration:
                stages.remove(gen)


def _recur_prompt(qb, lf, kk, ib, gb, uc, gc, lw, shared, c):
    b, t, width = qb.shape
    w_all, mk = shared['hgrn_consts']
    levels = mk.shape[0] - 1
    tok = pl.BlockSpec((None, c, width), lambda bi, ci: (bi, ci, 0))
    per_b = lambda shp: pl.BlockSpec((None,) + shp, lambda bi, ci: (bi,) + (0,) * len(shp))
    ml_names = ['conv_w', 'conv_b', 'wq', 'wk', 'wv', 'wg', 'bg', 'skip', 'mng']
    hg_w = [lw['hng'], w_all, mk, shared['e256']]
    ml_w = [lw[n] for n in ml_names] + [shared['mlstm_tri'], shared['e256']]
    hg_specs = [lw.spec('hng')] + [_const_spec(w.shape) for w in hg_w[1:]]
    ml_specs = [lw.spec(n) for n in ml_names] + [_const_spec(w.shape) for w in ml_w[len(ml_names):]]
    assert 5 + len(hg_w) == N_HGRN_IN and 2 + len(ml_w) == N_MLSTM_IN
    state = (N_HEADS_B, HEAD_DIM, HEAD_DIM)
    out_shape = [jax.ShapeDtypeStruct(qb.shape, F32), jax.ShapeDtypeStruct((b,) + state, F32),
                 jax.ShapeDtypeStruct(uc.shape, F32), jax.ShapeDtypeStruct((b,) + state, F32),
                 jax.ShapeDtypeStruct((b, 1, width), F32), jax.ShapeDtypeStruct((b, 1, GATE_LANES), F32),
                 jax.ShapeDtypeStruct((b, SUBLANES, width), F32)]
    return pl.pallas_call(
        functools.partial(_recur_kernel, c, levels),
        grid=(b, t // c),
        in_specs=[tok] * 5 + hg_specs + [tok] * 2 + ml_specs,
        out_specs=[tok, per_b(state), tok, per_b(state), per_b((1, width)), per_b((1, GATE_LANES)),
                   per_b((SUBLANES, width))],
        out_shape=out_shape,
        scratch_shapes=[pltpu.VMEM((width, width), F32),
                        pltpu.VMEM((c + SUBLANES, width), F32), pltpu.VMEM((width, width), F32),
                        pltpu.VMEM((1, width), F32), pltpu.VMEM((1, GATE_LANES), F32)],
        compiler_params=_params("arbitrary", "arbitrary"),
        name="recur_prompt",
    )(qb, lf, kk, ib, gb, *hg_w, uc, gc, *ml_w)


K_UNROLL = 8


def _hgrn_sample_kernel(t_new, q_ref, kk_ref, v_ref, g_ref, s0_ref, ng_ref, o_ref, sn_ref):
    sn_ref[...] = s0_ref[...]
    for t in range(t_new):
        vt = v_ref[t]

        def kbody(kb, o, t=t):
            for kk in range(K_UNROLL):
                k = kb * K_UNROLL + kk
                kt = kk_ref[t, pl.ds(k, 1), :]
                s_k = (1.0 - kt) * sn_ref[k] + kt * vt
                sn_ref[k] = s_k
                o = o + s_k * q_ref[t, pl.ds(k, 1), :]
            return o

        o = lax.fori_loop(0, HEAD_DIM // K_UNROLL, kbody, jnp.zeros(vt.shape, F32))
        ss = jnp.sum(o * o, axis=0, keepdims=True) * (1.0 / HEAD_DIM)
        o_ref[t] = o * lax.rsqrt(ss + EPS) * ng_ref[...] * g_ref[t]


def _hgrn_sample(qt, kkt, vt, gt, s_all, lw):
    t_new, width, b = qt.shape
    layer = lw.layer
    tok = pl.BlockSpec((t_new, HEAD_DIM, b), lambda h: (0, h, 0))
    st_in = pl.BlockSpec((None, None, HEAD_DIM, HEAD_DIM, b), lambda h: (layer, h, 0, 0, 0))
    st = pl.BlockSpec((None, HEAD_DIM, HEAD_DIM, b), lambda h: (h, 0, 0, 0))
    return pl.pallas_call(
        functools.partial(_hgrn_sample_kernel, t_new),
        grid=(N_HEADS_B,),
        in_specs=[tok, tok, tok, tok, st_in, lw.head_col_spec('hng_col')],
        out_specs=[tok, st],
        out_shape=[jax.ShapeDtypeStruct(qt.shape, F32), jax.ShapeDtypeStruct(s_all.shape[1:], F32)],
        compiler_params=_params("arbitrary"),
        name="hgrn_sample",
    )(qt, kkt, vt, gt, s_all, lw['hng_col'])


def _mlstm_front_kernel(t_new, uc_ref, cv0_ref, cw_ref, cb_ref, wq_ref, wk_ref, wv_ref, wi_ref, wf_ref,
                        bi_ref, bf_ref, uconv_ref, qm_ref, km_ref, vm_ref, i_ref, lf_ref):
    hist = [cv0_ref[j] for j in range(CONV_W - 1)] + [uc_ref[t] for t in range(t_new)]
    for t in range(t_new):
        conv = cb_ref[...]
        for j in range(CONV_W):
            conv = conv + cw_ref[j] * hist[t + j]
        uconv = _silu(conv)
        uconv_ref[t] = uconv
        qm = _dot(wq_ref[...], uconv)
        km = _dot(wk_ref[...], uconv)
        vm = _dot(wv_ref[...], hist[t + CONV_W - 1])
        qm_ref[t] = qm
        km_ref[t] = km
        vm_ref[t] = vm
        i_ref[t] = _dot3(wi_ref[0], qm) + _dot3(wi_ref[1], km) + _dot3(wi_ref[2], vm) + bi_ref[...]
        lf_ref[t] = _log_sigmoid(_dot3(wf_ref[0], qm) + _dot3(wf_ref[1], km) + _dot3(wf_ref[2], vm)
                                 + bf_ref[...])


def _mlstm_front(uct, cv_all, lw):
    t_new, width, b = uct.shape
    layer = lw.layer
    names = ['conv_w_col', 'conv_b_col', 'wq_t', 'wk_t', 'wv_t', 'wi_t', 'wf_t', 'bi_col', 'bf_col']
    big = jax.ShapeDtypeStruct(uct.shape, F32)
    small = jax.ShapeDtypeStruct((t_new, SUBLANES, b), F32)
    cv_spec = pl.BlockSpec((None,) + cv_all.shape[1:], lambda i: (layer, 0, 0, 0))
    return pl.pallas_call(
        functools.partial(_mlstm_front_kernel, t_new),
        grid=(1,),
        in_specs=[_const_spec(uct.shape), cv_spec] + [lw.spec(n) for n in names],
        out_specs=[_const_spec(uct.shape)] * 4 + [_const_spec(small.shape)] * 2,
        out_shape=[big] * 4 + [small] * 2,
        compiler_params=_params("arbitrary"),
        name="mlstm_front",
    )(uct, cv_all, *[lw[n] for n in names])


def _mlstm_sample_kernel(t_new, qm_ref, km_ref, vm_ref, i_ref, lf_ref, uconv_ref, g_ref, c0_ref, n0_ref, m0_ref,
                         skip_ref, ng_ref, o_ref, cn_ref, nn_ref, mn_ref):
    h = pl.program_id(0)
    cn_ref[...] = c0_ref[...]
    n = n0_ref[...]
    m = m0_ref[...]
    for t in range(t_new):
        i_t = i_ref[t, pl.ds(h, 1), :]
        lf_t = lf_ref[t, pl.ds(h, 1), :]
        m_new = jnp.maximum(lf_t + m, i_t)
        fp = jnp.exp(lf_t + m - m_new)
        ip = jnp.exp(i_t - m_new)
        vt = vm_ref[t]
        n = fp * n + ip * (km_ref[t] * QK_SCALE)

        def kbody(kb, num, t=t, fp=fp, ip=ip, vt=vt):
            for kk in range(K_UNROLL):
                k = kb * K_UNROLL + kk
                kt = km_ref[t, pl.ds(k, 1), :] * QK_SCALE
                c_k = fp * cn_ref[k] + (ip * kt) * vt
                cn_ref[k] = c_k
                num = num + c_k * qm_ref[t, pl.ds(k, 1), :]
            return num

        num = lax.fori_loop(0, HEAD_DIM // K_UNROLL, kbody, jnp.zeros(vt.shape, F32))
        den = jnp.sum(qm_ref[t] * n, axis=0, keepdims=True)
        hval = num / jnp.maximum(jnp.abs(den), jnp.exp(-m_new))
        m = m_new
        mean = jnp.sum(hval, axis=0, keepdims=True) * (1.0 / HEAD_DIM)
        xc = hval - mean
        var = jnp.sum(xc * xc, axis=0, keepdims=True) * (1.0 / HEAD_DIM)
        hc = xc * lax.rsqrt(var + EPS) * ng_ref[...]
        o_ref[t] = (hc + skip_ref[...] * uconv_ref[t]) * g_ref[t]
    nn_ref[...] = n
    mn_ref[...] = m


def _mlstm_sample(front, gt, c_all, n_all, m_all, lw):
    uconv, qm, km, vm, i_raw, logf = front
    t_new, width, b = qm.shape
    layer = lw.layer
    tok = pl.BlockSpec((t_new, HEAD_DIM, b), lambda h: (0, h, 0))
    gates = _const_spec(i_raw.shape)
    st = pl.BlockSpec((None, HEAD_DIM, HEAD_DIM, b), lambda h: (h, 0, 0, 0))
    nst = pl.BlockSpec((None, HEAD_DIM, b), lambda h: (h, 0, 0))
    mst = pl.BlockSpec((None, 1, b), lambda h: (h, 0, 0))
    st_in = pl.BlockSpec((None, None, HEAD_DIM, HEAD_DIM, b), lambda h: (layer, h, 0, 0, 0))
    nst_in = pl.BlockSpec((None, None, HEAD_DIM, b), lambda h: (layer, h, 0, 0))
    mst_in = pl.BlockSpec((None, None, 1, b), lambda h: (layer, h, 0, 0))
    return pl.pallas_call(
        functools.partial(_mlstm_sample_kernel, t_new),
        grid=(N_HEADS_C,),
        in_specs=[tok, tok, tok, gates, gates, tok, tok, st_in, nst_in, mst_in,
                  lw.head_col_spec('skip_col'), lw.head_col_spec('mng_col')],
        out_specs=[tok, st, nst, mst],
        out_shape=[jax.ShapeDtypeStruct(qm.shape, F32), jax.ShapeDtypeStruct(c_all.shape[1:], F32),
                   jax.ShapeDtypeStruct(n_all.shape[1:], F32), jax.ShapeDtypeStruct(m_all.shape[1:], F32)],
        compiler_params=_params("arbitrary"),
        name="mlstm_sample",
    )(qm, km, vm, i_raw, logf, uconv, gt, c_all, n_all, m_all, lw['skip_col'], lw['mng_col'])


class _LayerWeights:
    def __init__(self, stacked, layer):
        self.stacked = stacked
        self.layer = layer

    def __getitem__(self, name):
        return self.stacked[name]

    def spec(self, name):
        arr = self.stacked[name]
        layer = self.layer
        nd = arr.ndim - 1
        return pl.BlockSpec((None,) + arr.shape[1:], lambda *_: (layer,) + (0,) * nd)

    def head_col_spec(self, name):
        layer = self.layer
        return pl.BlockSpec((None, HEAD_DIM, 1), lambda h: (layer, h, 0))


def _stacked_weights(norm_g, w_in, q_norm_g, k_norm_g, hgrn_norm_g, mlstm_conv_w, mlstm_conv_b, mlstm_wq,
                     mlstm_wk, mlstm_wv, mlstm_w_ig, mlstm_b_ig, mlstm_w_fg, mlstm_b_fg, mlstm_skip, mlstm_norm_g,
                     w_out):
    depth = w_in.shape[0]
    eye = jnp.eye(N_HEADS_C, dtype=F32)

    def bd(w):
        return jnp.einsum('lhde,hg->lhdge', w, eye).reshape(depth, D_C, D_C).astype(BF16)

    def gate3(w):
        return w.reshape(depth, 3, D_C, N_HEADS_C)

    def gate_t(w):
        return jnp.pad(jnp.swapaxes(gate3(w), 2, 3), ((0, 0), (0, 0), (0, SUBLANES - N_HEADS_C), (0, 0)))

    def col8(v):
        return jnp.pad(v, ((0, 0), (0, SUBLANES - N_HEADS_C)))[:, :, None]

    wg = jnp.concatenate([gate3(mlstm_w_ig), gate3(mlstm_w_fg)], axis=-1)
    bg = jnp.concatenate([mlstm_b_ig, mlstm_b_fg], axis=-1)
    return {
        'norm_g': norm_g[:, None, :],
        'w_in': w_in.astype(BF16),
        'qg': jnp.tile(q_norm_g, (1, N_HEADS_A))[:, None, :],
        'kg': jnp.tile(k_norm_g, (1, N_HEADS_A))[:, None, :],
        'hng': hgrn_norm_g[:, None, :],
        'hng_col': hgrn_norm_g[:, :, None],
        'conv_w': mlstm_conv_w,
        'conv_b': mlstm_conv_b[:, None, :],
        'conv_w_col': mlstm_conv_w[:, :, :, None],
        'conv_b_col': mlstm_conv_b[:, :, None],
        'wq': bd(mlstm_wq), 'wk': bd(mlstm_wk), 'wv': bd(mlstm_wv),
        'wq_t': bd(jnp.swapaxes(mlstm_wq, 2, 3)), 'wk_t': bd(jnp.swapaxes(mlstm_wk, 2, 3)),
        'wv_t': bd(jnp.swapaxes(mlstm_wv, 2, 3)),
        'wg': jnp.pad(wg, ((0, 0), (0, 0), (0, 0), (0, GATE_LANES - 2 * N_HEADS_C))),
        'wi_t': gate_t(mlstm_w_ig), 'wf_t': gate_t(mlstm_w_fg),
        'bg': jnp.pad(bg, ((0, 0), (0, GATE_LANES - 2 * N_HEADS_C)))[:, None, :],
        'bi_col': col8(mlstm_b_ig), 'bf_col': col8(mlstm_b_fg),
        'skip': mlstm_skip[:, None, :], 'mng': mlstm_norm_g[:, None, :],
        'skip_col': mlstm_skip[:, :, None], 'mng_col': mlstm_norm_g[:, :, None],
        'wo_a': w_out[:, :D_A].astype(BF16),
        'wo_b': w_out[:, D_A:D_A + D_B].astype(BF16),
        'wo_c': w_out[:, D_A + D_B:].astype(BF16),
    }


def _prompt_layer(x, layer, lw, shared, chunk, tm, depth, kv_prev):
    b, t, d = x.shape
    x2d = x.reshape(b * t, d)
    (qa, kt, vt, ga, qb, lf, kk, ib, gb, uc, gc) = _inproj(x2d, layer, lw, shared, tm, seq_len=t, depth=depth,
                                                           kv_prev=kv_prev)
    r3 = lambda a: a.reshape(b, t, a.shape[-1])
    oa = _moba_prompt(r3(qa), kt, vt, r3(ga), shared['slopes'], layer)
    ob, s_new, oc, c_new, n_new, m_new, cv_new = _recur_prompt(r3(qb), r3(lf), r3(kk), r3(ib), r3(gb), r3(uc),
                                                               r3(gc), lw, shared, chunk)
    y = _outproj(x2d, oa.reshape(b * t, D_A), ob.reshape(b * t, D_B), oc.reshape(b * t, D_C), lw, tm)
    return (y.reshape(b, t, d), kt, vt, s_new, c_new, n_new.reshape(b, N_HEADS_C, HEAD_DIM),
            m_new[:, 0, :N_HEADS_C], cv_new[:, SUBLANES - (CONV_W - 1):, :])


def _sample_layer(x, layer, lw, shared, cache_kt, cache_vt, pt_flat, s_all, c_all, n_all, m_all, cv_all, tm):
    b, t, d = x.shape
    n_pages = pt_flat.shape[0] // b
    x2d = x.reshape(b * t, d)
    (qa, ka, va, ga, qb, lf, kk, ib, gb, uc, gc) = _inproj(x2d, layer, lw, shared, tm)
    del lf
    oa = _moba_sample(qa, ka, va, ga, cache_kt, cache_vt, pt_flat, layer, shared['sample_consts'], b, t, n_pages)
    to_lanes = lambda a: jnp.transpose(a.reshape(b, t, a.shape[-1]), (1, 2, 0))
    from_lanes = lambda a: jnp.transpose(a, (2, 0, 1)).reshape(b * t, a.shape[1])
    obt, s_new = _hgrn_sample(to_lanes(qb), to_lanes(kk), to_lanes(ib), to_lanes(gb), s_all, lw)
    front = _mlstm_front(to_lanes(uc), cv_all, lw)
    oct, c_new, n_new, m_new = _mlstm_sample(front, to_lanes(gc), c_all, n_all, m_all, lw)
    y = _outproj(x2d, oa, from_lanes(obt), from_lanes(oct), lw, tm)
    conv_new = uc.reshape(b, t, D_C)[:, t - (CONV_W - 1):, :]
    return (y.reshape(b, t, d), ka.reshape(b, t, N_HEADS_A, HEAD_DIM), va.reshape(b, t, N_HEADS_A, HEAD_DIM),
            s_new, c_new, n_new, m_new, conv_new)


def kernel(x_prompt, x_sample, cache_k, cache_v, page_table, state_hgrn, state_mlstm_c, state_mlstm_n,
           state_mlstm_m, state_mlstm_conv, norm_g, w_in, q_norm_g, k_norm_g, hgrn_lb, hgrn_norm_g,
           mlstm_conv_w, mlstm_conv_b, mlstm_wq, mlstm_wk, mlstm_wv, mlstm_w_ig, mlstm_b_ig, mlstm_w_fg,
           mlstm_b_fg, mlstm_skip, mlstm_norm_g, w_out):
    depth = w_in.shape[0]
    bp, tp, _ = x_prompt.shape
    bd, td, _ = x_sample.shape
    n_pages = page_table.shape[1]
    page = cache_k.shape[2]
    chunk = min(tp, 256)
    shared = {
        'e256': _head_block_ones(),
        'hgrn_lb': hgrn_lb.astype(F32),
        'hgrn_consts': _hgrn_consts(chunk),
        'mlstm_tri': _mlstm_consts(chunk),
        'slopes': jnp.asarray(2.0 ** (-8.0 * (np.arange(N_HEADS_A) + 1) / N_HEADS_A), dtype=F32),
        'sample_consts': _moba_sample_consts(td, n_pages * page),
    }
    pt_flat = page_table.reshape(-1).astype(jnp.int32)
    cache_kt = jnp.transpose(cache_k, (0, 1, 3, 4, 2))
    cache_vt = jnp.transpose(cache_v, (0, 1, 3, 4, 2))
    s_h = jnp.transpose(state_hgrn, (0, 2, 3, 4, 1))
    s_c = jnp.transpose(state_mlstm_c, (0, 2, 3, 4, 1))
    s_n = jnp.transpose(state_mlstm_n, (0, 2, 3, 1))
    s_m = jnp.transpose(state_mlstm_m, (0, 2, 1))[:, :, None, :]
    s_cv = jnp.transpose(state_mlstm_conv, (0, 2, 3, 1))

    yp, ys = x_prompt, x_sample
    kv_prompt_buf = None
    outs_p = [[] for _ in range(7)]
    outs_s = [[] for _ in range(7)]
    stacked = _stacked_weights(norm_g, w_in, q_norm_g, k_norm_g, hgrn_norm_g, mlstm_conv_w, mlstm_conv_b,
                               mlstm_wq, mlstm_wk, mlstm_wv, mlstm_w_ig, mlstm_b_ig, mlstm_w_fg, mlstm_b_fg,
                               mlstm_skip, mlstm_norm_g, w_out)
    for l in range(depth):
        lw = _LayerWeights(stacked, l)
        res_p = _prompt_layer(yp, l, lw, shared, chunk, min(PROMPT_ROW_TILE, tp), depth, kv_prompt_buf)
        res_s = _sample_layer(ys, l, lw, shared, cache_kt, cache_vt, pt_flat, s_h, s_c, s_n, s_m, s_cv,
                              SAMPLE_ROW_TILE)
        yp, ys = res_p[0], res_s[0]
        kv_prompt_buf = (res_p[1], res_p[2])
        for acc, a in zip(outs_p[2:], res_p[3:]):
            acc.append(a)
        for acc, a in zip(outs_s, res_s[1:]):
            acc.append(a)

    st = lambda lst, ax: jnp.stack(lst, axis=ax)

    def kv_prompt(buf):
        a = buf.reshape(bp, depth, N_HEADS_A, HEAD_DIM, tp)
        return jnp.transpose(a, (0, 1, 4, 2, 3))

    batch_first = lambda a: jnp.moveaxis(a, -1, 1)
    return (yp, ys, kv_prompt(kv_prompt_buf[0]), kv_prompt(kv_prompt_buf[1]), st(outs_s[0], 1), st(outs_s[1], 1),
            st(outs_p[2], 0), batch_first(st(outs_s[2], 0)), st(outs_p[3], 0), batch_first(st(outs_s[3], 0)),
            st(outs_p[4], 0), batch_first(st(outs_s[4], 0)), st(outs_p[5], 0),
            batch_first(st(outs_s[5], 0)[:, :, 0, :]), st(outs_p[6], 0), st(outs_s[6], 0))
```

```python
import functools
import math

import numpy as np
import jax
import jax.numpy as jnp
from jax import lax
from jax.experimental import pallas as pl
from jax.experimental.pallas import tpu as pltpu

F32 = jnp.float32
BF16 = jnp.bfloat16

HEAD_DIM = 64
N_HEADS_A = 8
N_HEADS_B = 4
N_HEADS_C = 4
D_A = N_HEADS_A * HEAD_DIM
D_B = N_HEADS_B * HEAD_DIM
D_C = N_HEADS_C * HEAD_DIM
MOBA_BLOCK = 256
MOBA_TOPK = 3
CONV_W = 4
EPS = 1e-6
NEG = -1e30
GATE_LANES = 128
LANES = 128
SUBLANES = 8
VMEM_LIMIT = 56 * 1024 * 1024
PROMPT_ROW_TILE = 512
SAMPLE_ROW_TILE = 256
QK_SCALE = HEAD_DIM ** -0.5
LOG2E = math.log2(math.e)


def _bf(x):
    return x.astype(BF16)


def _dot(a, b):
    return jnp.dot(_bf(a), _bf(b), preferred_element_type=F32)


def _dot_nt(a, b):
    return lax.dot_general(_bf(a), _bf(b), (((1,), (1,)), ((), ())), preferred_element_type=F32)


def _dot_tn(a, b):
    return lax.dot_general(_bf(a), _bf(b), (((0,), (0,)), ((), ())), preferred_element_type=F32)


def _split2(x):
    hi = _bf(x)
    lo = _bf(x - hi.astype(F32))
    return hi, lo


def _split3(x):
    hi = _bf(x)
    r = x - hi.astype(F32)
    mid = _bf(r)
    lo = _bf(r - mid.astype(F32))
    return hi, mid, lo


def _dot_sel(w01, x, parts=3):
    pieces = _split3(x)[:parts]
    out = jnp.dot(w01, pieces[0], preferred_element_type=F32)
    for p in pieces[1:]:
        out = out + jnp.dot(w01, p, preferred_element_type=F32)
    return out


def _dot3(a, b):
    ah, al = _split2(a)
    bh, bl = _split2(b)
    return (jnp.dot(ah, bh, preferred_element_type=F32) + jnp.dot(al, bh, preferred_element_type=F32)
            + jnp.dot(ah, bl, preferred_element_type=F32))


def _dot3_tn(a, b):
    ah, al = _split2(a)
    bh, bl = _split2(b)
    dn = (((0,), (0,)), ((), ()))
    return (lax.dot_general(ah, bh, dn, preferred_element_type=F32)
            + lax.dot_general(al, bh, dn, preferred_element_type=F32)
            + lax.dot_general(ah, bl, dn, preferred_element_type=F32))


def _seg_sum(x, e_ref):
    hi, lo = _split2(x)
    e = e_ref[...]
    return jnp.dot(hi, e, preferred_element_type=F32) + jnp.dot(lo, e, preferred_element_type=F32)


def _spread_heads(x, rows, n_heads):
    lane = lax.broadcasted_iota(jnp.int32, (rows, n_heads * HEAD_DIM), 1)
    out = jnp.zeros((rows, n_heads * HEAD_DIM), F32)
    for h in range(n_heads):
        in_head = (lane >= h * HEAD_DIM) & (lane < (h + 1) * HEAD_DIM)
        out = jnp.where(in_head, x[:, h:h + 1], out)
    return out


def _silu(x):
    return x * jax.nn.sigmoid(x)


def _log_sigmoid(x):
    return jnp.minimum(x, 0.0) - jnp.log(1.0 + jnp.exp(-jnp.abs(x)))


def _topk_rows(g, n_rows, limit):
    rid = lax.broadcasted_iota(jnp.int32, g.shape, 0)
    g = jnp.where(rid < limit, g, -jnp.inf)
    cnt = jnp.zeros(g.shape, jnp.int32)
    for m in range(n_rows):
        row = g[m:m + 1, :]
        beats = (row > g) | ((row == g) & (m < rid))
        cnt = cnt + beats.astype(jnp.int32)
    return ((cnt < MOBA_TOPK) & (rid < limit)).astype(F32)


def _const_spec(shape):
    nd = len(shape)
    return pl.BlockSpec(shape, lambda *_: (0,) * nd)


def _head_block_ones():
    r = np.arange(256)
    return jnp.asarray((r[:, None] // HEAD_DIM) == (r[None, :] // HEAD_DIM), dtype=BF16)


def _block_diag(w):
    h = w.shape[0]
    eye = jnp.eye(h, dtype=w.dtype)
    return jnp.einsum('hde,hg->hdge', w, eye).reshape(h * HEAD_DIM, h * HEAD_DIM)


def _params(*sem):
    return pltpu.CompilerParams(dimension_semantics=sem, vmem_limit_bytes=VMEM_LIMIT)


def _inproj_kernel(layer, kv_transposed, x_ref, g_ref, w_ref, qg_ref, kg_ref, lb_ref, e_ref,
                   qa_ref, ka_ref, va_ref, ga_ref, qb_ref, lf_ref, kk_ref, ib_ref, gb_ref, uc_ref, gc_ref):
    x = x_ref[...]
    h = x * lax.rsqrt(jnp.mean(x * x, axis=-1, keepdims=True) + EPS) * g_ref[...]
    hb = _bf(h)

    def proj(c0, width):
        return jnp.dot(hb, w_ref[:, c0:c0 + width], preferred_element_type=F32)

    def head_rms(p, g):
        halves = [_seg_sum(p[:, c:c + 256] * p[:, c:c + 256], e_ref) for c in (0, 256)]
        ss = jnp.concatenate(halves, axis=1)
        return p * lax.rsqrt(ss * (1.0 / HEAD_DIM) + EPS) * g

    qa_ref[...] = head_rms(proj(0, D_A), qg_ref[...])
    ka = head_rms(proj(D_A, D_A), kg_ref[...])
    va = proj(2 * D_A, D_A)
    if kv_transposed:
        ka_ref[...] = ka.T
        va_ref[...] = va.T
    else:
        ka_ref[...] = ka
        va_ref[...] = va
    ga_ref[...] = _silu(proj(3 * D_A, D_A))
    c = 4 * D_A
    qb_ref[...] = proj(c, D_B)
    lbp = lb_ref[...]
    lbe = jnp.exp(lbp - jnp.max(lbp, axis=0, keepdims=True))
    lbw = lbe / jnp.sum(lbe, axis=0, keepdims=True)
    lb_cum = lbw[0:1, :]
    for j in range(1, layer + 1):
        lb_cum = lb_cum + lbw[j:j + 1, :]
    lb = lb_cum - lbw[0:1, :]
    fg = lb + (1.0 - lb) * jax.nn.sigmoid(proj(c + D_B, D_B))
    lf_ref[...] = jnp.log(fg)
    kk_ref[...] = 1.0 - fg
    ib_ref[...] = proj(c + 2 * D_B, D_B)
    gb_ref[...] = _silu(proj(c + 3 * D_B, D_B))
    c = c + 4 * D_B
    uc_ref[...] = proj(c, D_C)
    gc_ref[...] = _silu(proj(c + D_C, D_C))


N_INPROJ_IN = 7


def _inproj_kernel_stacked(layer, *refs):
    _inproj_kernel(layer, True, *refs[:N_INPROJ_IN], *refs[N_INPROJ_IN + 2:])


def _inproj(x2d, layer, lw, shared, tm, seq_len=None, depth=None, kv_prev=None):
    m, d = x2d.shape
    kv_transposed = seq_len is not None
    widths = [D_A] * 4 + [D_B] * 5 + [D_C] * 2
    out_shape = [jax.ShapeDtypeStruct((m, w), F32) for w in widths]
    out_specs = [pl.BlockSpec((tm, w), lambda i: (i, 0)) for w in widths]
    if kv_transposed:
        tiles = seq_len // tm
        for idx in (1, 2):
            out_shape[idx] = jax.ShapeDtypeStruct((m // seq_len, depth, D_A, seq_len), F32)
            out_specs[idx] = pl.BlockSpec((None, None, D_A, tm), lambda i: (i // tiles, layer, 0, i % tiles))
    in_specs = [pl.BlockSpec((tm, d), lambda i: (i, 0)),
                lw.spec('norm_g'), lw.spec('w_in'), lw.spec('qg'), lw.spec('kg'),
                _const_spec(shared['hgrn_lb'].shape), _const_spec((256, 256))]
    args = [x2d, lw['norm_g'], lw['w_in'], lw['qg'], lw['kg'], shared['hgrn_lb'], shared['e256']]
    assert len(args) == N_INPROJ_IN
    body = functools.partial(_inproj_kernel, layer, kv_transposed)
    aliases = {}
    if kv_prev is not None:
        in_specs += [pl.BlockSpec(memory_space=pl.ANY)] * 2
        args += list(kv_prev)
        aliases = {N_INPROJ_IN: 1, N_INPROJ_IN + 1: 2}
        body = functools.partial(_inproj_kernel_stacked, layer)
    return pl.pallas_call(
        body,
        grid=(m // tm,),
        in_specs=in_specs,
        out_specs=out_specs,
        out_shape=out_shape,
        input_output_aliases=aliases,
        compiler_params=_params("arbitrary"),
        name="inproj",
    )(*args)


def _outproj_kernel(x_ref, oa_ref, ob_ref, oc_ref, wa_ref, wb_ref, wc_ref, y_ref):
    y_ref[...] = (x_ref[...] + _dot(oa_ref[...], wa_ref[...]) + _dot(ob_ref[...], wb_ref[...])
                  + _dot(oc_ref[...], wc_ref[...]))


def _outproj(x2d, oa, ob, oc, lw, tm):
    m, d = x2d.shape
    row = lambda w: pl.BlockSpec((tm, w), lambda i: (i, 0))
    return pl.pallas_call(
        _outproj_kernel,
        grid=(m // tm,),
        in_specs=[row(d), row(D_A), row(D_B), row(D_C), lw.spec('wo_a'), lw.spec('wo_b'), lw.spec('wo_c')],
        out_specs=row(d),
        out_shape=jax.ShapeDtypeStruct((m, d), F32),
        compiler_params=_params("arbitrary"),
        name="outproj",
    )(x2d, oa, ob, oc, lw['wo_a'], lw['wo_b'], lw['wo_c'])


HEADS_PER_STEP = 8


def _moba_prompt_kernel(nb, slopes_ref, q_ref, kt_ref, vt_ref, g_ref, o_ref, kmt_ref, sel_ref, sd_ref, sdo_ref,
                        raw_ref):
    g = pl.program_id(1)
    i = pl.program_id(2)
    blk = MOBA_BLOCK
    nh = HEADS_PER_STEP
    width = nh * HEAD_DIM

    @pl.when(i == 0)
    def _():
        lane = lax.broadcasted_iota(jnp.int32, (width, LANES), 1)
        km = jnp.zeros((width, LANES), F32)
        for n in range(nb):
            col = jnp.sum(kt_ref[:, n * blk:(n + 1) * blk], axis=1, keepdims=True) * (1.0 / blk)
            km = jnp.where(lane == n, col, km)
        km_rows = jnp.concatenate([km.T[0:nb]] * nh, axis=0)
        r_head = lax.broadcasted_iota(jnp.int32, (nh * nb, width), 0) // nb
        c_head = lax.broadcasted_iota(jnp.int32, (nh * nb, width), 1) // HEAD_DIM
        km_hi, km_lo = _split2(jnp.where(r_head == c_head, km_rows, 0.0))
        kmt_ref[0] = km_hi.astype(BF16)
        kmt_ref[1] = km_lo.astype(BF16)
        tq = lax.broadcasted_iota(jnp.int32, (blk, blk), 1)
        tk = lax.broadcasted_iota(jnp.int32, (blk, blk), 0)
        d0 = (tq - tk).astype(F32)
        for h in range(nh):
            sd = (slopes_ref[nh * g + h] * LOG2E) * d0
            sd_ref[h] = sd
            sdo_ref[h] = jnp.where(d0 >= 0.0, sd, -NEG)

    qt = q_ref[...].T
    qts = _bf(qt * (QK_SCALE * LOG2E))

    def scores(h, start):
        r0 = h * HEAD_DIM
        kb = kt_ref[r0:r0 + HEAD_DIM, pl.ds(start, blk)]
        return _dot_tn(kb, qts[r0:r0 + HEAD_DIM])

    q_hi, q_lo = _split2(qt)
    km_hi = kmt_ref[0]
    gates = (jnp.dot(km_hi, q_hi, preferred_element_type=F32) + jnp.dot(kmt_ref[1], q_hi, preferred_element_type=F32)
             + jnp.dot(km_hi, q_lo, preferred_element_type=F32))
    for h in range(nh):
        sel_ref[h] = _topk_rows(gates[h * nb:(h + 1) * nb], nb, i)

    def head_step(h, carry, start, sp, shift, keep=None):
        m_run, l_run, acc = carry
        r0 = h * HEAD_DIM
        m_blk = jnp.max(sp, axis=0, keepdims=True) - shift
        off = shift
        if keep is not None:
            m_blk = jnp.where(keep > 0.5, m_blk, NEG)
            off = jnp.where(keep > 0.5, shift, -4.0 * NEG)
        m_new = jnp.maximum(m_run, m_blk)
        alpha = jnp.exp2(m_run - m_new)
        pt = jnp.exp2(sp - (m_new + off))
        l_new = alpha * l_run + jnp.sum(pt, axis=0, keepdims=True)
        vb = vt_ref[r0:r0 + HEAD_DIM, pl.ds(start, blk)]
        acc_new = alpha * acc + _dot(vb, pt)
        return m_new, l_new, acc_new

    def past(n, carries):
        start = pl.multiple_of(n * blk, blk)
        nxt = pl.multiple_of((n + 1) * blk, blk)
        gap = ((i - n) * blk).astype(F32) * LOG2E
        out = []
        for h in range(nh):
            sp = raw_ref[h] - sd_ref[h]
            raw_ref[h] = scores(h, nxt)
            out.append(head_step(h, carries[h], start, sp, slopes_ref[nh * g + h] * gap,
                                 keep=sel_ref[h, pl.ds(n, 1), :]))
        return tuple(out)

    init = tuple((jnp.full((1, blk), NEG, F32), jnp.zeros((1, blk), F32), jnp.zeros((HEAD_DIM, blk), F32))
                 for _ in range(nh))
    for h in range(nh):
        raw_ref[h] = scores(h, 0)
    carries = lax.fori_loop(0, i, past, init)
    start = pl.multiple_of(i * blk, blk)
    outs = []
    for h in range(nh):
        _, l_fin, acc = head_step(h, carries[h], start, raw_ref[h] - sdo_ref[h], 0.0)
        outs.append(acc / l_fin)
    o_ref[...] = jnp.concatenate(outs, axis=0).T * g_ref[...]


def _moba_prompt(q, kt, vt, gate, slopes, layer):
    b, t, _ = q.shape
    nb = t // MOBA_BLOCK
    width = HEADS_PER_STEP * HEAD_DIM
    qspec = pl.BlockSpec((None, MOBA_BLOCK, width), lambda bi, g, i: (bi, i, g))
    kspec = pl.BlockSpec((None, None, width, t), lambda bi, g, i: (bi, layer, g, 0))
    return pl.pallas_call(
        functools.partial(_moba_prompt_kernel, nb),
        grid=(b, N_HEADS_A // HEADS_PER_STEP, nb),
        in_specs=[pl.BlockSpec(memory_space=pltpu.SMEM), qspec, kspec, kspec, qspec],
        out_specs=qspec,
        out_shape=jax.ShapeDtypeStruct(q.shape, F32),
        scratch_shapes=[pltpu.VMEM((2, HEADS_PER_STEP * nb, width), BF16),
                        pltpu.VMEM((HEADS_PER_STEP, nb, MOBA_BLOCK), F32),
                        pltpu.VMEM((HEADS_PER_STEP, MOBA_BLOCK, MOBA_BLOCK), F32),
                        pltpu.VMEM((HEADS_PER_STEP, MOBA_BLOCK, MOBA_BLOCK), F32),
                        pltpu.VMEM((HEADS_PER_STEP, MOBA_BLOCK, MOBA_BLOCK), F32)],
        compiler_params=_params("arbitrary", "arbitrary", "arbitrary"),
        name="moba_prompt",
    )(slopes, q, kt, vt, gate)


SCORE_ROWS = 64
SEQS_PER_STEP = 2


def _moba_sample_kernel(n_pages, page, t_new, pt_ref, q_ref, kn_ref, vn_ref, g_ref, rowc_ref, hm_ref, *rest):
    n_ops = SEQS_PER_STEP * n_pages
    kp_refs = rest[:n_ops]
    vp_refs = rest[n_ops:2 * n_ops]
    o_ref = rest[2 * n_ops]
    s_ref = rest[2 * n_ops + 1]
    del pt_ref
    stages = [_moba_sample_seq(n_pages, page, t_new, s, q_ref, kn_ref, vn_ref, g_ref, rowc_ref, hm_ref,
                               kp_refs[s * n_pages:(s + 1) * n_pages], vp_refs[s * n_pages:(s + 1) * n_pages],
                               o_ref, s_ref)
              for s in range(SEQS_PER_STEP)]
    while stages:
        for gen in list(stages):
            if next(gen, StopIteration) is StopIteration:
                stages.remove(gen)


def _moba_sample_seq(n_pages, page, t_new, slot, q_ref, kn_ref, vn_ref, g_ref, rowc_ref, hm_ref, kp_refs, vp_refs,
                     o_ref, s_ref):
    tok = slice(slot * t_new, (slot + 1) * t_new)
    s_ref = s_ref.at[slot]
    nh = N_HEADS_A
    rows = SCORE_ROWS
    pages_per_blk = MOBA_BLOCK // page
    nblk = n_pages // pages_per_blk
    hm = hm_ref[...]
    q = q_ref[tok, :]
    qrep = jnp.concatenate([q] * (rows // t_new), axis=0) * hm
    qsb = _bf(qrep * QK_SCALE)
    slope = rowc_ref[:, 0:1]
    qpos = rowc_ref[:, 1:2]
    lane_f = lax.broadcasted_iota(jnp.int32, (rows, page), 1).astype(F32)

    lane_g = lax.broadcasted_iota(jnp.int32, (rows, LANES), 1)
    gates = jnp.zeros((rows, LANES), F32)
    for j in range(n_pages):
        kp = kp_refs[j][...].reshape(nh * HEAD_DIM, page)
        st = jnp.dot(qsb, _bf(kp), preferred_element_type=F32)
        gates = gates + jnp.where(lane_g == j // pages_per_blk, jnp.sum(st, axis=1, keepdims=True), 0.0)
        dist = (qpos - float(j * page)) - lane_f
        s_ref[:, j * page:(j + 1) * page] = st - slope * dist
    yield

    cnt = jnp.zeros((rows, LANES), jnp.int32)
    for m in range(nblk):
        col = gates[:, m:m + 1]
        cnt = cnt + ((col > gates) | ((col == gates) & (m < lane_g))).astype(jnp.int32)
    sel_t = (cnt < MOBA_TOPK).astype(F32)

    s_own = _dot_nt(qsb, kn_ref[tok, :])
    trow = lax.broadcasted_iota(jnp.int32, (rows, t_new), 0) % t_new
    tcol = lax.broadcasted_iota(jnp.int32, (rows, t_new), 1)
    dist_own = (trow - tcol).astype(F32)
    s_own = jnp.where(dist_own >= 0.0, s_own - slope * dist_own, NEG)
    yield

    mvec = jnp.full((rows, page), NEG, F32)
    for j in range(n_pages):
        n = j // pages_per_blk
        st = jnp.where(sel_t[:, n:n + 1] > 0.5, s_ref[:, j * page:(j + 1) * page], NEG)
        s_ref[:, j * page:(j + 1) * page] = st
        mvec = jnp.maximum(mvec, st)
    m_row = jnp.maximum(jnp.max(mvec, axis=1, keepdims=True), jnp.max(s_own, axis=1, keepdims=True))
    yield

    p_own = jnp.exp(s_own - m_row)
    lvec = jnp.zeros((rows, page), F32)
    acc = _dot(p_own, vn_ref[tok, :])
    for j in range(n_pages):
        pj = jnp.exp(s_ref[:, j * page:(j + 1) * page] - m_row)
        lvec = lvec + pj
        acc = acc + _dot_nt(pj, vp_refs[j][...].reshape(nh * HEAD_DIM, page))
    yield
    l_row = jnp.sum(lvec, axis=1, keepdims=True) + jnp.sum(p_own, axis=1, keepdims=True)
    tot = acc * (1.0 / l_row) * hm
    out = tot[0:t_new]
    for h in range(1, nh):
        out = out + tot[h * t_new:(h + 1) * t_new]
    o_ref[tok, :] = out * g_ref[tok, :]


def _moba_sample(q, k_new, v_new, gate, cache_kt, cache_vt, pt_flat, layer, consts, n_seq, t_new, n_pages):
    page = cache_kt.shape[-1]
    rowc, hm = consts
    per = SEQS_PER_STEP
    tspec = pl.BlockSpec((per * t_new, D_A), lambda b, pt: (b, 0))

    def page_spec(s, j):
        return pl.BlockSpec((None, None, N_HEADS_A, HEAD_DIM, page),
                            lambda b, pt, s=s, j=j: (pt[(b * per + s) * n_pages + j], layer, 0, 0, 0))

    cspec = lambda a: pl.BlockSpec(a.shape, lambda b, pt: (0,) * a.ndim)
    pages = [page_spec(s, j) for s in range(per) for j in range(n_pages)]
    grid_spec = pltpu.PrefetchScalarGridSpec(
        num_scalar_prefetch=1,
        grid=(n_seq // per,),
        in_specs=[tspec, tspec, tspec, tspec, cspec(rowc), cspec(hm)] + pages + pages,
        out_specs=tspec,
        scratch_shapes=[pltpu.VMEM((per, SCORE_ROWS, n_pages * page), F32)],
    )
    return pl.pallas_call(
        functools.partial(_moba_sample_kernel, n_pages, page, t_new),
        grid_spec=grid_spec,
        out_shape=jax.ShapeDtypeStruct(q.shape, F32),
        compiler_params=_params("arbitrary"),
        name="moba_sample",
    )(pt_flat, q, k_new, v_new, gate, rowc, hm, *([cache_kt] * (per * n_pages)), *([cache_vt] * (per * n_pages)))


def _moba_sample_consts(t_new, past_len):
    nh = N_HEADS_A
    used = nh * t_new
    r = np.arange(SCORE_ROWS)
    live = r < used
    slopes = 2.0 ** (-8.0 * (np.arange(nh) + 1) / nh)
    rowc = np.zeros((SCORE_ROWS, 2), np.float32)
    rowc[:, 0] = np.where(live, slopes[np.minimum(r // t_new, nh - 1)], 0.0)
    rowc[:, 1] = past_len + (r % t_new)
    c = np.arange(nh * HEAD_DIM)
    hm = (((c[None, :] // HEAD_DIM) == (r[:, None] // t_new)) & live[:, None]).astype(np.float32)
    return jnp.asarray(rowc), jnp.asarray(hm)


def _hgrn_consts(c):
    levels = int(round(math.log2(c)))
    t = np.arange(c)[:, None]
    u = np.arange(c)[None, :]
    mats = [u <= t]
    masks = [t == u]
    for j in range(1, levels + 1):
        p = 2 ** j
        hlf = p // 2
        mid = (t // p) * p + hlf
        upper = (t % p) >= hlf
        mats.append((upper & (u >= mid) & (u <= t)) | ((~upper) & (u > t) & (u <= mid - 1)))
        masks.append(((t // p) == (u // p)) & upper & ((u % p) < hlf))
    w_all = jnp.asarray(np.concatenate(mats, axis=0), dtype=BF16)
    mk = jnp.asarray(np.stack(masks), dtype=F32)
    return w_all, mk


def _hgrn_kernel(c, levels, q_ref, lf_ref, kk_ref, v_ref, g_ref, ng_ref, w_ref, mk_ref, e_ref,
                 o_ref, sn_ref, sbd_ref):
    nh = N_HEADS_B
    width = nh * HEAD_DIM
    lf = lf_ref[...]
    d_all = _dot_sel(w_ref[...], lf, parts=2)
    b = d_all[0:c]
    e_b = jnp.exp(b)
    e_end = jnp.exp(b[c - 1:c, :] - b)
    q = q_ref[...]
    k = kk_ref[...]
    v = v_ref[...]
    vb = _bf(v)
    sbd = sbd_ref[...]
    o = _dot(q * e_b, sbd)
    yield
    lane = lax.broadcasted_iota(jnp.int32, (c, width), 1)
    qk_levels = [(q, k)]
    for j in range(1, levels + 1):
        e_j = jnp.exp(d_all[j * c:(j + 1) * c])
        qk_levels.append((q * e_j, k * e_j))
    heads = [(lane >= h * HEAD_DIM) & (lane < (h + 1) * HEAD_DIM) for h in range(nh)]
    kbs = [_bf(kj) for _, kj in qk_levels]
    a_heads = []
    for h in range(nh):
        a = jnp.zeros((c, c), F32)
        for j, (qj, _) in enumerate(qk_levels):
            a = a + _dot_nt(jnp.where(heads[h], qj, 0.0), kbs[j]) * mk_ref[j]
        a_heads.append(_bf(a))
        yield
    for h in range(nh):
        o = o + jnp.where(heads[h], jnp.dot(a_heads[h], vb, preferred_element_type=F32), 0.0)

    yield
    b_end_col = jnp.broadcast_to(b[c - 1:c, :], (SUBLANES, width)).T[:, 0:1]
    decay = jnp.exp(b_end_col)
    r = lax.broadcasted_iota(jnp.int32, (width, width), 0) // HEAD_DIM
    cc = lax.broadcasted_iota(jnp.int32, (width, width), 1) // HEAD_DIM
    s_new = sbd * decay + jnp.where(r == cc, _dot_tn(k * e_end, v), 0.0)
    sbd_ref[...] = s_new
    yield

    ss = _seg_sum(o * o, e_ref)
    o_ref[...] = o * lax.rsqrt(ss * (1.0 / HEAD_DIM) + EPS) * ng_ref[...] * g_ref[...]
    for h in range(nh):
        sn_ref[h] = s_new[h * HEAD_DIM:(h + 1) * HEAD_DIM, h * HEAD_DIM:(h + 1) * HEAD_DIM]


def _mlstm_consts(c):
    t = np.arange(c)[:, None]
    u = np.arange(c)[None, :]
    return jnp.asarray(u <= t, dtype=BF16)


def _mlstm_kernel(c, uc_ref, g_ref, cw_ref, cb_ref, wq_ref, wk_ref, wv_ref,
                  wg_ref, bg_ref, skip_ref, ng_ref, tri_ref, e_ref,
                  o_ref, cn_ref, nn_ref, mn_ref, cvn_ref,
                  uext_ref, cbd_ref, n_ref, m_ref):
    nh = N_HEADS_C
    width = nh * HEAD_DIM

    uc = uc_ref[...]
    uext_ref[8:8 + c, :] = uc
    conv = cb_ref[...]
    for j in range(CONV_W):
        conv = conv + cw_ref[j:j + 1, :] * uext_ref[5 + j:5 + j + c, :]
    tail = uext_ref[c:c + 8, :]
    uext_ref[0:8, :] = tail
    uconv = _silu(conv)

    qm = _dot(uconv, wq_ref[...])
    km = _dot(uconv, wk_ref[...])
    vm = _dot(uc, wv_ref[...])
    i_raw = _dot3(qm, wg_ref[0]) + _dot3(km, wg_ref[1]) + _dot3(vm, wg_ref[2]) + bg_ref[...]
    logf = pltpu.roll(_log_sigmoid(i_raw), GATE_LANES - N_HEADS_C, 1)
    yield
    bcum = _dot_sel(tri_ref[...], logf)
    a = i_raw - bcum
    rowi = lax.broadcasted_iota(jnp.int32, (c, GATE_LANES), 0)
    s = 1
    while s < c:
        a = jnp.maximum(a, jnp.where(rowi >= s, pltpu.roll(a, s, 0), -jnp.inf))
        s *= 2
    m0 = m_ref[...]
    m_t = bcum + jnp.maximum(m0, a)
    g_in = jnp.exp(bcum + m0 - m_t)
    bm = bcum - m_t
    ib_t = (i_raw - bcum).T
    m_end = m_t[c - 1:c, :]
    b_end = bcum[c - 1:c, :]
    w_tok = jnp.exp((b_end - bcum) + i_raw - m_end)
    g_end = jnp.exp(b_end + m0 - m_end)

    yield
    ks = km * QK_SCALE
    ksb = _bf(ks)
    vmb = _bf(vm)
    lane = lax.broadcasted_iota(jnp.int32, (c, width), 1)
    trow = lax.broadcasted_iota(jnp.int32, (c, c), 0)
    tcol = lax.broadcasted_iota(jnp.int32, (c, c), 1)
    causal = tcol <= trow
    g256 = _spread_heads(g_in, c, nh)
    cbd = cbd_ref[...]
    n0 = n_ref[...]
    num = g256 * _dot(qm, cbd)
    qn = qm * n0
    den_cols = jnp.zeros((c, GATE_LANES), F32)
    lane_g = lax.broadcasted_iota(jnp.int32, (c, GATE_LANES), 1)
    heads = [(lane >= h * HEAD_DIM) & (lane < (h + 1) * HEAD_DIM) for h in range(nh)]
    raw = [_dot_nt(jnp.where(heads[h], qm, 0.0), ksb) for h in range(nh)]
    for h in range(nh):
        expo = bm[:, h:h + 1] + ib_t[h:h + 1, :]
        dmat = jnp.exp(jnp.where(causal, expo, NEG))
        qk = raw[h] * dmat
        num = num + jnp.where(heads[h], jnp.dot(_bf(qk), vmb, preferred_element_type=F32), 0.0)
        den_h = (g_in[:, h:h + 1] * jnp.sum(jnp.where(heads[h], qn, 0.0), axis=1, keepdims=True)
                 + jnp.sum(qk, axis=1, keepdims=True))
        den_cols = jnp.where(lane_g == h, den_h, den_cols)
        yield
    denom = jnp.maximum(jnp.abs(den_cols), jnp.exp(-m_t))
    hval = num / _spread_heads(denom, c, nh)

    w256 = _spread_heads(w_tok, c, nh)
    gend256 = _spread_heads(g_end, 1, nh)
    r = lax.broadcasted_iota(jnp.int32, (width, width), 0) // HEAD_DIM
    cc = lax.broadcasted_iota(jnp.int32, (width, width), 1) // HEAD_DIM
    kw = ks * w256
    c_new = cbd * gend256 + jnp.where(r == cc, _dot_tn(kw, vm), 0.0)
    n_new = gend256 * n0 + jnp.sum(kw, axis=0, keepdims=True)
    cbd_ref[...] = c_new
    n_ref[...] = n_new
    m_ref[...] = m_end
    yield

    mean = _seg_sum(hval, e_ref) * (1.0 / HEAD_DIM)
    xc = hval - mean
    var = _seg_sum(xc * xc, e_ref) * (1.0 / HEAD_DIM)
    hc = xc * lax.rsqrt(var + EPS) * ng_ref[...]
    o_ref[...] = (hc + skip_ref[...] * uconv) * g_ref[...]
    for h in range(nh):
        cn_ref[h] = c_new[h * HEAD_DIM:(h + 1) * HEAD_DIM, h * HEAD_DIM:(h + 1) * HEAD_DIM]
    nn_ref[...] = n_new
    mn_ref[...] = m_end
    cvn_ref[...] = tail


N_HGRN_IN, N_HGRN_OUT, N_HGRN_SCRATCH = 9, 2, 1
N_MLSTM_IN, N_MLSTM_OUT, N_MLSTM_SCRATCH = 13, 5, 4


def _recur_kernel(c, levels, *refs):
    i0 = 0
    hg_in = refs[i0:i0 + N_HGRN_IN]
    i0 += N_HGRN_IN
    ml_in = refs[i0:i0 + N_MLSTM_IN]
    i0 += N_MLSTM_IN
    hg_out = refs[i0:i0 + N_HGRN_OUT]
    i0 += N_HGRN_OUT
    ml_out = refs[i0:i0 + N_MLSTM_OUT]
    i0 += N_MLSTM_OUT
    hg_scr = refs[i0:i0 + N_HGRN_SCRATCH]
    i0 += N_HGRN_SCRATCH
    ml_scr = refs[i0:i0 + N_MLSTM_SCRATCH]

    @pl.when(pl.program_id(1) == 0)
    def _():
        for ref in hg_scr + ml_scr:
            ref[...] = jnp.zeros(ref.shape, F32)

    stages = [_hgrn_kernel(c, levels, *hg_in, *hg_out, *hg_scr), _mlstm_kernel(c, *ml_in, *ml_out, *ml_scr)]
    while stages:
        for gen in list(stages):
            if next(gen, StopIteration) is StopIteration:
                stages.remove(gen)


def _recur_prompt(qb, lf, kk, ib, gb, uc, gc, lw, shared, c):
    b, t, width = qb.shape
    w_all, mk = shared['hgrn_consts']
    levels = mk.shape[0] - 1
    tok = pl.BlockSpec((None, c, width), lambda bi, ci: (bi, ci, 0))
    per_b = lambda shp: pl.BlockSpec((None,) + shp, lambda bi, ci: (bi,) + (0,) * len(shp))
    ml_names = ['conv_w', 'conv_b', 'wq', 'wk', 'wv', 'wg', 'bg', 'skip', 'mng']
    hg_w = [lw['hng'], w_all, mk, shared['e256']]
    ml_w = [lw[n] for n in ml_names] + [shared['mlstm_tri'], shared['e256']]
    hg_specs = [lw.spec('hng')] + [_const_spec(w.shape) for w in hg_w[1:]]
    ml_specs = [lw.spec(n) for n in ml_names] + [_const_spec(w.shape) for w in ml_w[len(ml_names):]]
    assert 5 + len(hg_w) == N_HGRN_IN and 2 + len(ml_w) == N_MLSTM_IN
    state = (N_HEADS_B, HEAD_DIM, HEAD_DIM)
    out_shape = [jax.ShapeDtypeStruct(qb.shape, F32), jax.ShapeDtypeStruct((b,) + state, F32),
                 jax.ShapeDtypeStruct(uc.shape, F32), jax.ShapeDtypeStruct((b,) + state, F32),
                 jax.ShapeDtypeStruct((b, 1, width), F32), jax.ShapeDtypeStruct((b, 1, GATE_LANES), F32),
                 jax.ShapeDtypeStruct((b, SUBLANES, width), F32)]
    return pl.pallas_call(
        functools.partial(_recur_kernel, c, levels),
        grid=(b, t // c),
        in_specs=[tok] * 5 + hg_specs + [tok] * 2 + ml_specs,
        out_specs=[tok, per_b(state), tok, per_b(state), per_b((1, width)), per_b((1, GATE_LANES)),
                   per_b((SUBLANES, width))],
        out_shape=out_shape,
        scratch_shapes=[pltpu.VMEM((width, width), F32),
                        pltpu.VMEM((c + SUBLANES, width), F32), pltpu.VMEM((width, width), F32),
                        pltpu.VMEM((1, width), F32), pltpu.VMEM((1, GATE_LANES), F32)],
        compiler_params=_params("arbitrary", "arbitrary"),
        name="recur_prompt",
    )(qb, lf, kk, ib, gb, *hg_w, uc, gc, *ml_w)


def _recur_attn_kernel(c, levels, n_pages, page, t_new, pt_ref, *refs):
    del pt_ref
    n_ops = SEQS_PER_STEP * n_pages
    n_rec_in = N_HGRN_IN + N_MLSTM_IN
    n_att_in = 6 + 2 * n_ops
    n_rec_out = N_HGRN_OUT + N_MLSTM_OUT
    rec_in = refs[:n_rec_in]
    att_in = refs[n_rec_in:n_rec_in + n_att_in]
    outs = refs[n_rec_in + n_att_in:n_rec_in + n_att_in + n_rec_out + 1]
    scr = refs[n_rec_in + n_att_in + n_rec_out + 1:]
    hg_in, ml_in = rec_in[:N_HGRN_IN], rec_in[N_HGRN_IN:]
    hg_out, ml_out, o_ref = outs[:N_HGRN_OUT], outs[N_HGRN_OUT:n_rec_out], outs[n_rec_out]
    hg_scr, ml_scr = scr[:N_HGRN_SCRATCH], scr[N_HGRN_SCRATCH:N_HGRN_SCRATCH + N_MLSTM_SCRATCH]
    s_ref = scr[N_HGRN_SCRATCH + N_MLSTM_SCRATCH]
    q_ref, kn_ref, vn_ref, g_ref, rowc_ref, hm_ref = att_in[:6]
    kp_refs, vp_refs = att_in[6:6 + n_ops], att_in[6 + n_ops:]

    @pl.when(pl.program_id(1) == 0)
    def _():
        for ref in hg_scr + ml_scr:
            ref[...] = jnp.zeros(ref.shape, F32)

    stages = [_hgrn_kernel(c, levels, *hg_in, *hg_out, *hg_scr), _mlstm_kernel(c, *ml_in, *ml_out, *ml_scr)]
    stages += [_moba_sample_seq(n_pages, page, t_new, s, q_ref, kn_ref, vn_ref, g_ref, rowc_ref, hm_ref,
                                kp_refs[s * n_pages:(s + 1) * n_pages], vp_refs[s * n_pages:(s + 1) * n_pages],
                                o_ref, s_ref)
               for s in range(SEQS_PER_STEP)]
    while stages:
        for gen in list(stages):
            if next(gen, StopIteration) is StopIteration:
                stages.remove(gen)


def _recur_prompt_with_sample_attention(qb, lf, kk, ib, gb, uc, gc, lw, shared, c,
                                        q, k_new, v_new, gate, cache_kt, cache_vt, pt_flat, n_seq, t_new, n_pages):
    b, t, width = qb.shape
    nc = t // c
    per = SEQS_PER_STEP
    assert b * nc * per == n_seq
    layer = lw.layer
    page = cache_kt.shape[-1]
    rowc, hm = shared['sample_consts']
    w_all, mk = shared['hgrn_consts']
    levels = mk.shape[0] - 1
    tok = pl.BlockSpec((None, c, width), lambda bi, ci, pt: (bi, ci, 0))
    per_b = lambda shp: pl.BlockSpec((None,) + shp, lambda bi, ci, pt: (bi,) + (0,) * len(shp))
    ml_names = ['conv_w', 'conv_b', 'wq', 'wk', 'wv', 'wg', 'bg', 'skip', 'mng']
    hg_w = [lw['hng'], w_all, mk, shared['e256']]
    ml_w = [lw[n] for n in ml_names] + [shared['mlstm_tri'], shared['e256']]
    hg_specs = [lw.spec('hng')] + [_const_spec(w.shape) for w in hg_w[1:]]
    ml_specs = [lw.spec(n) for n in ml_names] + [_const_spec(w.shape) for w in ml_w[len(ml_names):]]
    tspec = pl.BlockSpec((per * t_new, D_A), lambda bi, ci, pt: (bi * nc + ci, 0))

    def page_spec(s, j):
        return pl.BlockSpec((None, None, N_HEADS_A, HEAD_DIM, page),
                            lambda bi, ci, pt, s=s, j=j: (pt[((bi * nc + ci) * per + s) * n_pages + j], layer, 0, 0, 0))

    pages = [page_spec(s, j) for s in range(per) for j in range(n_pages)]
    state = (N_HEADS_B, HEAD_DIM, HEAD_DIM)
    out_shape = [jax.ShapeDtypeStruct(qb.shape, F32), jax.ShapeDtypeStruct((b,) + state, F32),
                 jax.ShapeDtypeStruct(uc.shape, F32), jax.ShapeDtypeStruct((b,) + state, F32),
                 jax.ShapeDtypeStruct((b, 1, width), F32), jax.ShapeDtypeStruct((b, 1, GATE_LANES), F32),
                 jax.ShapeDtypeStruct((b, SUBLANES, width), F32), jax.ShapeDtypeStruct(q.shape, F32)]
    grid_spec = pltpu.PrefetchScalarGridSpec(
        num_scalar_prefetch=1,
        grid=(b, nc),
        in_specs=[tok] * 5 + hg_specs + [tok] * 2 + ml_specs
        + [tspec] * 4 + [_const_spec(rowc.shape), _const_spec(hm.shape)] + pages + pages,
        out_specs=[tok, per_b(state), tok, per_b(state), per_b((1, width)), per_b((1, GATE_LANES)),
                   per_b((SUBLANES, width)), tspec],
        scratch_shapes=[pltpu.VMEM((width, width), F32),
                        pltpu.VMEM((c + SUBLANES, width), F32), pltpu.VMEM((width, width), F32),
                        pltpu.VMEM((1, width), F32), pltpu.VMEM((1, GATE_LANES), F32),
                        pltpu.VMEM((per, SCORE_ROWS, n_pages * page), F32)],
    )
    return pl.pallas_call(
        functools.partial(_recur_attn_kernel, c, levels, n_pages, page, t_new),
        grid_spec=grid_spec,
        out_shape=out_shape,
        compiler_params=_params("arbitrary", "arbitrary"),
        name="recur_prompt_attn_sample",
    )(pt_flat, qb, lf, kk, ib, gb, *hg_w, uc, gc, *ml_w, q, k_new, v_new, gate, rowc, hm,
      *([cache_kt] * (per * n_pages)), *([cache_vt] * (per * n_pages)))


K_UNROLL = 8


def _hgrn_sample_kernel(t_new, q_ref, kk_ref, v_ref, g_ref, s0_ref, ng_ref, o_ref, sn_ref):
    sn_ref[...] = s0_ref[...]
    for t in range(t_new):
        vt = v_ref[t]

        def kbody(kb, o, t=t):
            for kk in range(K_UNROLL):
                k = kb * K_UNROLL + kk
                kt = kk_ref[t, pl.ds(k, 1), :]
                s_k = (1.0 - kt) * sn_ref[k] + kt * vt
                sn_ref[k] = s_k
                o = o + s_k * q_ref[t, pl.ds(k, 1), :]
            return o

        o = lax.fori_loop(0, HEAD_DIM // K_UNROLL, kbody, jnp.zeros(vt.shape, F32))
        ss = jnp.sum(o * o, axis=0, keepdims=True) * (1.0 / HEAD_DIM)
        o_ref[t] = o * lax.rsqrt(ss + EPS) * ng_ref[...] * g_ref[t]


def _hgrn_sample(qt, kkt, vt, gt, s_all, lw):
    t_new, width, b = qt.shape
    layer = lw.layer
    tok = pl.BlockSpec((t_new, HEAD_DIM, b), lambda h: (0, h, 0))
    st_in = pl.BlockSpec((None, None, HEAD_DIM, HEAD_DIM, b), lambda h: (layer, h, 0, 0, 0))
    st = pl.BlockSpec((None, HEAD_DIM, HEAD_DIM, b), lambda h: (h, 0, 0, 0))
    return pl.pallas_call(
        functools.partial(_hgrn_sample_kernel, t_new),
        grid=(N_HEADS_B,),
        in_specs=[tok, tok, tok, tok, st_in, lw.head_col_spec('hng_col')],
        out_specs=[tok, st],
        out_shape=[jax.ShapeDtypeStruct(qt.shape, F32), jax.ShapeDtypeStruct(s_all.shape[1:], F32)],
        compiler_params=_params("arbitrary"),
        name="hgrn_sample",
    )(qt, kkt, vt, gt, s_all, lw['hng_col'])


def _mlstm_front_kernel(t_new, uc_ref, cv0_ref, cw_ref, cb_ref, wq_ref, wk_ref, wv_ref, wi_ref, wf_ref,
                        bi_ref, bf_ref, uconv_ref, qm_ref, km_ref, vm_ref, i_ref, lf_ref):
    hist = [cv0_ref[j] for j in range(CONV_W - 1)] + [uc_ref[t] for t in range(t_new)]
    for t in range(t_new):
        conv = cb_ref[...]
        for j in range(CONV_W):
            conv = conv + cw_ref[j] * hist[t + j]
        uconv = _silu(conv)
        uconv_ref[t] = uconv
        qm = _dot(wq_ref[...], uconv)
        km = _dot(wk_ref[...], uconv)
        vm = _dot(wv_ref[...], hist[t + CONV_W - 1])
        qm_ref[t] = qm
        km_ref[t] = km
        vm_ref[t] = vm
        i_ref[t] = _dot3(wi_ref[0], qm) + _dot3(wi_ref[1], km) + _dot3(wi_ref[2], vm) + bi_ref[...]
        lf_ref[t] = _log_sigmoid(_dot3(wf_ref[0], qm) + _dot3(wf_ref[1], km) + _dot3(wf_ref[2], vm)
                                 + bf_ref[...])


def _mlstm_front(uct, cv_all, lw):
    t_new, width, b = uct.shape
    layer = lw.layer
    names = ['conv_w_col', 'conv_b_col', 'wq_t', 'wk_t', 'wv_t', 'wi_t', 'wf_t', 'bi_col', 'bf_col']
    big = jax.ShapeDtypeStruct(uct.shape, F32)
    small = jax.ShapeDtypeStruct((t_new, SUBLANES, b), F32)
    cv_spec = pl.BlockSpec((None,) + cv_all.shape[1:], lambda i: (layer, 0, 0, 0))
    return pl.pallas_call(
        functools.partial(_mlstm_front_kernel, t_new),
        grid=(1,),
        in_specs=[_const_spec(uct.shape), cv_spec] + [lw.spec(n) for n in names],
        out_specs=[_const_spec(uct.shape)] * 4 + [_const_spec(small.shape)] * 2,
        out_shape=[big] * 4 + [small] * 2,
        compiler_params=_params("arbitrary"),
        name="mlstm_front",
    )(uct, cv_all, *[lw[n] for n in names])


def _mlstm_sample_kernel(t_new, qm_ref, km_ref, vm_ref, i_ref, lf_ref, uconv_ref, g_ref, c0_ref, n0_ref, m0_ref,
                         skip_ref, ng_ref, o_ref, cn_ref, nn_ref, mn_ref):
    h = pl.program_id(0)
    cn_ref[...] = c0_ref[...]
    n = n0_ref[...]
    m = m0_ref[...]
    for t in range(t_new):
        i_t = i_ref[t, pl.ds(h, 1), :]
        lf_t = lf_ref[t, pl.ds(h, 1), :]
        m_new = jnp.maximum(lf_t + m, i_t)
        fp = jnp.exp(lf_t + m - m_new)
        ip = jnp.exp(i_t - m_new)
        vt = vm_ref[t]
        n = fp * n + ip * (km_ref[t] * QK_SCALE)

        def kbody(kb, num, t=t, fp=fp, ip=ip, vt=vt):
            for kk in range(K_UNROLL):
                k = kb * K_UNROLL + kk
                kt = km_ref[t, pl.ds(k, 1), :] * QK_SCALE
                c_k = fp * cn_ref[k] + (ip * kt) * vt
                cn_ref[k] = c_k
                num = num + c_k * qm_ref[t, pl.ds(k, 1), :]
            return num

        num = lax.fori_loop(0, HEAD_DIM // K_UNROLL, kbody, jnp.zeros(vt.shape, F32))
        den = jnp.sum(qm_ref[t] * n, axis=0, keepdims=True)
        hval = num / jnp.maximum(jnp.abs(den), jnp.exp(-m_new))
        m = m_new
        mean = jnp.sum(hval, axis=0, keepdims=True) * (1.0 / HEAD_DIM)
        xc = hval - mean
        var = jnp.sum(xc * xc, axis=0, keepdims=True) * (1.0 / HEAD_DIM)
        hc = xc * lax.rsqrt(var + EPS) * ng_ref[...]
        o_ref[t] = (hc + skip_ref[...] * uconv_ref[t]) * g_ref[t]
    nn_ref[...] = n
    mn_ref[...] = m


def _mlstm_sample(front, gt, c_all, n_all, m_all, lw):
    uconv, qm, km, vm, i_raw, logf = front
    t_new, width, b = qm.shape
    layer = lw.layer
    tok = pl.BlockSpec((t_new, HEAD_DIM, b), lambda h: (0, h, 0))
    gates = _const_spec(i_raw.shape)
    st = pl.BlockSpec((None, HEAD_DIM, HEAD_DIM, b), lambda h: (h, 0, 0, 0))
    nst = pl.BlockSpec((None, HEAD_DIM, b), lambda h: (h, 0, 0))
    mst = pl.BlockSpec((None, 1, b), lambda h: (h, 0, 0))
    st_in = pl.BlockSpec((None, None, HEAD_DIM, HEAD_DIM, b), lambda h: (layer, h, 0, 0, 0))
    nst_in = pl.BlockSpec((None, None, HEAD_DIM, b), lambda h: (layer, h, 0, 0))
    mst_in = pl.BlockSpec((None, None, 1, b), lambda h: (layer, h, 0, 0))
    return pl.pallas_call(
        functools.partial(_mlstm_sample_kernel, t_new),
        grid=(N_HEADS_C,),
        in_specs=[tok, tok, tok, gates, gates, tok, tok, st_in, nst_in, mst_in,
                  lw.head_col_spec('skip_col'), lw.head_col_spec('mng_col')],
        out_specs=[tok, st, nst, mst],
        out_shape=[jax.ShapeDtypeStruct(qm.shape, F32), jax.ShapeDtypeStruct(c_all.shape[1:], F32),
                   jax.ShapeDtypeStruct(n_all.shape[1:], F32), jax.ShapeDtypeStruct(m_all.shape[1:], F32)],
        compiler_params=_params("arbitrary"),
        name="mlstm_sample",
    )(qm, km, vm, i_raw, logf, uconv, gt, c_all, n_all, m_all, lw['skip_col'], lw['mng_col'])


class _LayerWeights:
    def __init__(self, stacked, layer):
        self.stacked = stacked
        self.layer = layer

    def __getitem__(self, name):
        return self.stacked[name]

    def spec(self, name):
        arr = self.stacked[name]
        layer = self.layer
        nd = arr.ndim - 1
        return pl.BlockSpec((None,) + arr.shape[1:], lambda *_: (layer,) + (0,) * nd)

    def head_col_spec(self, name):
        layer = self.layer
        return pl.BlockSpec((None, HEAD_DIM, 1), lambda h: (layer, h, 0))


def _stacked_weights(norm_g, w_in, q_norm_g, k_norm_g, hgrn_norm_g, mlstm_conv_w, mlstm_conv_b, mlstm_wq,
                     mlstm_wk, mlstm_wv, mlstm_w_ig, mlstm_b_ig, mlstm_w_fg, mlstm_b_fg, mlstm_skip, mlstm_norm_g,
                     w_out):
    depth = w_in.shape[0]
    eye = jnp.eye(N_HEADS_C, dtype=F32)

    def bd(w):
        return jnp.einsum('lhde,hg->lhdge', w, eye).reshape(depth, D_C, D_C).astype(BF16)

    def gate3(w):
        return w.reshape(depth, 3, D_C, N_HEADS_C)

    def gate_t(w):
        return jnp.pad(jnp.swapaxes(gate3(w), 2, 3), ((0, 0), (0, 0), (0, SUBLANES - N_HEADS_C), (0, 0)))

    def col8(v):
        return jnp.pad(v, ((0, 0), (0, SUBLANES - N_HEADS_C)))[:, :, None]

    wg = jnp.concatenate([gate3(mlstm_w_ig), gate3(mlstm_w_fg)], axis=-1)
    bg = jnp.concatenate([mlstm_b_ig, mlstm_b_fg], axis=-1)
    return {
        'norm_g': norm_g[:, None, :],
        'w_in': w_in.astype(BF16),
        'qg': jnp.tile(q_norm_g, (1, N_HEADS_A))[:, None, :],
        'kg': jnp.tile(k_norm_g, (1, N_HEADS_A))[:, None, :],
        'hng': hgrn_norm_g[:, None, :],
        'hng_col': hgrn_norm_g[:, :, None],
        'conv_w': mlstm_conv_w,
        'conv_b': mlstm_conv_b[:, None, :],
        'conv_w_col': mlstm_conv_w[:, :, :, None],
        'conv_b_col': mlstm_conv_b[:, :, None],
        'wq': bd(mlstm_wq), 'wk': bd(mlstm_wk), 'wv': bd(mlstm_wv),
        'wq_t': bd(jnp.swapaxes(mlstm_wq, 2, 3)), 'wk_t': bd(jnp.swapaxes(mlstm_wk, 2, 3)),
        'wv_t': bd(jnp.swapaxes(mlstm_wv, 2, 3)),
        'wg': jnp.pad(wg, ((0, 0), (0, 0), (0, 0), (0, GATE_LANES - 2 * N_HEADS_C))),
        'wi_t': gate_t(mlstm_w_ig), 'wf_t': gate_t(mlstm_w_fg),
        'bg': jnp.pad(bg, ((0, 0), (0, GATE_LANES - 2 * N_HEADS_C)))[:, None, :],
        'bi_col': col8(mlstm_b_ig), 'bf_col': col8(mlstm_b_fg),
        'skip': mlstm_skip[:, None, :], 'mng': mlstm_norm_g[:, None, :],
        'skip_col': mlstm_skip[:, :, None], 'mng_col': mlstm_norm_g[:, :, None],
        'wo_a': w_out[:, :D_A].astype(BF16),
        'wo_b': w_out[:, D_A:D_A + D_B].astype(BF16),
        'wo_c': w_out[:, D_A + D_B:].astype(BF16),
    }


def _layer(xp, xs, layer, lw, shared, chunk, tm_p, tm_s, depth, kv_prev, cache_kt, cache_vt, pt_flat,
           s_all, c_all, n_all, m_all, cv_all):
    b, t, d = xp.shape
    bs, ts, _ = xs.shape
    n_pages = pt_flat.shape[0] // bs
    xp2d = xp.reshape(b * t, d)
    xs2d = xs.reshape(bs * ts, d)
    (qa, kt, vt, ga, qb, lf, kk, ib, gb, uc, gc) = _inproj(xp2d, layer, lw, shared, tm_p, seq_len=t, depth=depth,
                                                           kv_prev=kv_prev)
    (qa_s, ka_s, va_s, ga_s, qb_s, _, kk_s, ib_s, gb_s, uc_s, gc_s) = _inproj(xs2d, layer, lw, shared, tm_s)
    r3 = lambda a: a.reshape(b, t, a.shape[-1])
    oa = _moba_prompt(r3(qa), kt, vt, r3(ga), shared['slopes'], layer)
    rec_args = (r3(qb), r3(lf), r3(kk), r3(ib), r3(gb), r3(uc), r3(gc), lw, shared, chunk)
    if b * (t // chunk) * SEQS_PER_STEP == bs:
        ob, s_new, oc, c_new, n_new, m_new, cv_new, oa_s = _recur_prompt_with_sample_attention(
            *rec_args, qa_s, ka_s, va_s, ga_s, cache_kt, cache_vt, pt_flat, bs, ts, n_pages)
    else:
        ob, s_new, oc, c_new, n_new, m_new, cv_new = _recur_prompt(*rec_args)
        oa_s = _moba_sample(qa_s, ka_s, va_s, ga_s, cache_kt, cache_vt, pt_flat, layer, shared['sample_consts'],
                            bs, ts, n_pages)
    yp = _outproj(xp2d, oa.reshape(b * t, D_A), ob.reshape(b * t, D_B), oc.reshape(b * t, D_C), lw, tm_p)
    res_p = (yp.reshape(b, t, d), kt, vt, s_new, c_new, n_new.reshape(b, N_HEADS_C, HEAD_DIM),
             m_new[:, 0, :N_HEADS_C], cv_new[:, SUBLANES - (CONV_W - 1):, :])

    to_lanes = lambda a: jnp.transpose(a.reshape(bs, ts, a.shape[-1]), (1, 2, 0))
    from_lanes = lambda a: jnp.transpose(a, (2, 0, 1)).reshape(bs * ts, a.shape[1])
    obt, s_new_s = _hgrn_sample(to_lanes(qb_s), to_lanes(kk_s), to_lanes(ib_s), to_lanes(gb_s), s_all, lw)
    front = _mlstm_front(to_lanes(uc_s), cv_all, lw)
    oct, c_new_s, n_new_s, m_new_s = _mlstm_sample(front, to_lanes(gc_s), c_all, n_all, m_all, lw)
    ys = _outproj(xs2d, oa_s, from_lanes(obt), from_lanes(oct), lw, tm_s)
    conv_new = uc_s.reshape(bs, ts, D_C)[:, ts - (CONV_W - 1):, :]
    res_s = (ys.reshape(bs, ts, d), ka_s.reshape(bs, ts, N_HEADS_A, HEAD_DIM),
             va_s.reshape(bs, ts, N_HEADS_A, HEAD_DIM), s_new_s, c_new_s, n_new_s, m_new_s, conv_new)
    return res_p, res_s


def kernel(x_prompt, x_sample, cache_k, cache_v, page_table, state_hgrn, state_mlstm_c, state_mlstm_n,
           state_mlstm_m, state_mlstm_conv, norm_g, w_in, q_norm_g, k_norm_g, hgrn_lb, hgrn_norm_g,
           mlstm_conv_w, mlstm_conv_b, mlstm_wq, mlstm_wk, mlstm_wv, mlstm_w_ig, mlstm_b_ig, mlstm_w_fg,
           mlstm_b_fg, mlstm_skip, mlstm_norm_g, w_out):
    depth = w_in.shape[0]
    bp, tp, _ = x_prompt.shape
    bd, td, _ = x_sample.shape
    n_pages = page_table.shape[1]
    page = cache_k.shape[2]
    chunk = min(tp, 256)
    shared = {
        'e256': _head_block_ones(),
        'hgrn_lb': hgrn_lb.astype(F32),
        'hgrn_consts': _hgrn_consts(chunk),
        'mlstm_tri': _mlstm_consts(chunk),
        'slopes': jnp.asarray(2.0 ** (-8.0 * (np.arange(N_HEADS_A) + 1) / N_HEADS_A), dtype=F32),
        'sample_consts': _moba_sample_consts(td, n_pages * page),
    }
    pt_flat = page_table.reshape(-1).astype(jnp.int32)
    cache_kt = jnp.transpose(cache_k, (0, 1, 3, 4, 2))
    cache_vt = jnp.transpose(cache_v, (0, 1, 3, 4, 2))
    s_h = jnp.transpose(state_hgrn, (0, 2, 3, 4, 1))
    s_c = jnp.transpose(state_mlstm_c, (0, 2, 3, 4, 1))
    s_n = jnp.transpose(state_mlstm_n, (0, 2, 3, 1))
    s_m = jnp.transpose(state_mlstm_m, (0, 2, 1))[:, :, None, :]
    s_cv = jnp.transpose(state_mlstm_conv, (0, 2, 3, 1))

    yp, ys = x_prompt, x_sample
    kv_prompt_buf = None
    outs_p = [[] for _ in range(7)]
    outs_s = [[] for _ in range(7)]
    stacked = _stacked_weights(norm_g, w_in, q_norm_g, k_norm_g, hgrn_norm_g, mlstm_conv_w, mlstm_conv_b,
                               mlstm_wq, mlstm_wk, mlstm_wv, mlstm_w_ig, mlstm_b_ig, mlstm_w_fg, mlstm_b_fg,
                               mlstm_skip, mlstm_norm_g, w_out)
    for l in range(depth):
        lw = _LayerWeights(stacked, l)
        res_p, res_s = _layer(yp, ys, l, lw, shared, chunk, min(PROMPT_ROW_TILE, tp),
                              math.gcd(SAMPLE_ROW_TILE, bd * td), depth,
                              kv_prompt_buf, cache_kt, cache_vt, pt_flat, s_h, s_c, s_n, s_m, s_cv)
        yp, ys = res_p[0], res_s[0]
        kv_prompt_buf = (res_p[1], res_p[2])
        for acc, a in zip(outs_p[2:], res_p[3:]):
            acc.append(a)
        for acc, a in zip(outs_s, res_s[1:]):
            acc.append(a)

    st = lambda lst, ax: jnp.stack(lst, axis=ax)

    def kv_prompt(buf):
        a = buf.reshape(bp, depth, N_HEADS_A, HEAD_DIM, tp)
        return jnp.transpose(a, (0, 1, 4, 2, 3))

    batch_first = lambda a: jnp.moveaxis(a, -1, 1)
    return (yp, ys, kv_prompt(kv_prompt_buf[0]), kv_prompt(kv_prompt_buf[1]), st(outs_s[0], 1), st(outs_s[1], 1),
            st(outs_p[2], 0), batch_first(st(outs_s[2], 0)), st(outs_p[3], 0), batch_first(st(outs_s[3], 0)),
            st(outs_p[4], 0), batch_first(st(outs_s[4], 0)), st(outs_p[5], 0),
            batch_first(st(outs_s[5], 0)[:, :, 0, :]), st(outs_p[6], 0), st(outs_s[6], 0))
```

```python
import functools
import math

import numpy as np
import jax
import jax.numpy as jnp
from jax import lax
from jax.experimental import pallas as pl
from jax.experimental.pallas import tpu as pltpu

F32 = jnp.float32
BF16 = jnp.bfloat16

HEAD_DIM = 64
N_HEADS_A = 8
N_HEADS_B = 4
N_HEADS_C = 4
D_A = N_HEADS_A * HEAD_DIM
D_B = N_HEADS_B * HEAD_DIM
D_C = N_HEADS_C * HEAD_DIM
MOBA_BLOCK = 256
MOBA_TOPK = 3
CONV_W = 4
EPS = 1e-6
NEG = -1e30
GATE_LANES = 128
LANES = 128
SUBLANES = 8
VMEM_LIMIT = 56 * 1024 * 1024
PROMPT_ROW_TILE = 512
SAMPLE_ROW_TILE = 256
QK_SCALE = HEAD_DIM ** -0.5
LOG2E = math.log2(math.e)


def _bf(x):
    return x.astype(BF16)


def _dot(a, b):
    return jnp.dot(_bf(a), _bf(b), preferred_element_type=F32)


def _dot_nt(a, b):
    return lax.dot_general(_bf(a), _bf(b), (((1,), (1,)), ((), ())), preferred_element_type=F32)


def _dot_tn(a, b):
    return lax.dot_general(_bf(a), _bf(b), (((0,), (0,)), ((), ())), preferred_element_type=F32)


def _split2(x):
    hi = _bf(x)
    lo = _bf(x - hi.astype(F32))
    return hi, lo


def _split3(x):
    hi = _bf(x)
    r = x - hi.astype(F32)
    mid = _bf(r)
    lo = _bf(r - mid.astype(F32))
    return hi, mid, lo


def _dot_sel(w01, x, parts=3):
    pieces = _split3(x)[:parts]
    out = jnp.dot(w01, pieces[0], preferred_element_type=F32)
    for p in pieces[1:]:
        out = out + jnp.dot(w01, p, preferred_element_type=F32)
    return out


def _dot3(a, b):
    ah, al = _split2(a)
    bh, bl = _split2(b)
    return (jnp.dot(ah, bh, preferred_element_type=F32) + jnp.dot(al, bh, preferred_element_type=F32)
            + jnp.dot(ah, bl, preferred_element_type=F32))


def _dot3_tn(a, b):
    ah, al = _split2(a)
    bh, bl = _split2(b)
    dn = (((0,), (0,)), ((), ()))
    return (lax.dot_general(ah, bh, dn, preferred_element_type=F32)
            + lax.dot_general(al, bh, dn, preferred_element_type=F32)
            + lax.dot_general(ah, bl, dn, preferred_element_type=F32))


def _seg_sum(x, e_ref):
    hi, lo = _split2(x)
    e = e_ref[...]
    return jnp.dot(hi, e, preferred_element_type=F32) + jnp.dot(lo, e, preferred_element_type=F32)


def _spread_heads(x, rows, n_heads):
    lane = lax.broadcasted_iota(jnp.int32, (rows, n_heads * HEAD_DIM), 1)
    out = jnp.zeros((rows, n_heads * HEAD_DIM), F32)
    for h in range(n_heads):
        in_head = (lane >= h * HEAD_DIM) & (lane < (h + 1) * HEAD_DIM)
        out = jnp.where(in_head, x[:, h:h + 1], out)
    return out


def _silu(x):
    return x * jax.nn.sigmoid(x)


def _log_sigmoid(x):
    return jnp.minimum(x, 0.0) - jnp.log(1.0 + jnp.exp(-jnp.abs(x)))


def _topk_rows(g, n_rows, limit):
    rid = lax.broadcasted_iota(jnp.int32, g.shape, 0)
    g = jnp.where(rid < limit, g, -jnp.inf)
    cnt = jnp.zeros(g.shape, jnp.int32)
    for m in range(n_rows):
        row = g[m:m + 1, :]
        beats = (row > g) | ((row == g) & (m < rid))
        cnt = cnt + beats.astype(jnp.int32)
    return ((cnt < MOBA_TOPK) & (rid < limit)).astype(F32)


def _const_spec(shape):
    nd = len(shape)
    return pl.BlockSpec(shape, lambda *_: (0,) * nd)


def _head_block_ones():
    r = np.arange(256)
    return jnp.asarray((r[:, None] // HEAD_DIM) == (r[None, :] // HEAD_DIM), dtype=BF16)


def _block_diag(w):
    h = w.shape[0]
    eye = jnp.eye(h, dtype=w.dtype)
    return jnp.einsum('hde,hg->hdge', w, eye).reshape(h * HEAD_DIM, h * HEAD_DIM)


def _params(*sem):
    return pltpu.CompilerParams(dimension_semantics=sem, vmem_limit_bytes=VMEM_LIMIT)


def _inproj_kernel(layer, kv_transposed, x_ref, g_ref, w_ref, qg_ref, kg_ref, lb_ref, e_ref,
                   qa_ref, ka_ref, va_ref, ga_ref, qb_ref, lf_ref, kk_ref, ib_ref, gb_ref, uc_ref, gc_ref):
    x = x_ref[...]
    h = x * lax.rsqrt(jnp.mean(x * x, axis=-1, keepdims=True) + EPS) * g_ref[...]
    hb = _bf(h)

    def proj(c0, width):
        return jnp.dot(hb, w_ref[:, c0:c0 + width], preferred_element_type=F32)

    def head_rms(p, g):
        halves = [_seg_sum(p[:, c:c + 256] * p[:, c:c + 256], e_ref) for c in (0, 256)]
        ss = jnp.concatenate(halves, axis=1)
        return p * lax.rsqrt(ss * (1.0 / HEAD_DIM) + EPS) * g

    qa_ref[...] = head_rms(proj(0, D_A), qg_ref[...])
    ka = head_rms(proj(D_A, D_A), kg_ref[...])
    va = proj(2 * D_A, D_A)
    if kv_transposed:
        ka_ref[...] = ka.T
        va_ref[...] = va.T
    else:
        ka_ref[...] = ka
        va_ref[...] = va
    ga_ref[...] = _silu(proj(3 * D_A, D_A))
    c = 4 * D_A
    qb_ref[...] = proj(c, D_B)
    lbp = lb_ref[...]
    lbe = jnp.exp(lbp - jnp.max(lbp, axis=0, keepdims=True))
    lbw = lbe / jnp.sum(lbe, axis=0, keepdims=True)
    lb_cum = lbw[0:1, :]
    for j in range(1, layer + 1):
        lb_cum = lb_cum + lbw[j:j + 1, :]
    lb = lb_cum - lbw[0:1, :]
    fg = lb + (1.0 - lb) * jax.nn.sigmoid(proj(c + D_B, D_B))
    lf_ref[...] = jnp.log(fg)
    kk_ref[...] = 1.0 - fg
    ib_ref[...] = proj(c + 2 * D_B, D_B)
    gb_ref[...] = _silu(proj(c + 3 * D_B, D_B))
    c = c + 4 * D_B
    uc_ref[...] = proj(c, D_C)
    gc_ref[...] = _silu(proj(c + D_C, D_C))


N_INPROJ_IN = 7


def _inproj_kernel_stacked(layer, *refs):
    _inproj_kernel(layer, True, *refs[:N_INPROJ_IN], *refs[N_INPROJ_IN + 2:])


def _inproj(x2d, layer, lw, shared, tm, seq_len=None, depth=None, kv_prev=None):
    m, d = x2d.shape
    kv_transposed = seq_len is not None
    widths = [D_A] * 4 + [D_B] * 5 + [D_C] * 2
    out_shape = [jax.ShapeDtypeStruct((m, w), F32) for w in widths]
    out_specs = [pl.BlockSpec((tm, w), lambda i: (i, 0)) for w in widths]
    if kv_transposed:
        tiles = seq_len // tm
        for idx in (1, 2):
            out_shape[idx] = jax.ShapeDtypeStruct((m // seq_len, depth, D_A, seq_len), F32)
            out_specs[idx] = pl.BlockSpec((None, None, D_A, tm), lambda i: (i // tiles, layer, 0, i % tiles))
    in_specs = [pl.BlockSpec((tm, d), lambda i: (i, 0)),
                lw.spec('norm_g'), lw.spec('w_in'), lw.spec('qg'), lw.spec('kg'),
                _const_spec(shared['hgrn_lb'].shape), _const_spec((256, 256))]
    args = [x2d, lw['norm_g'], lw['w_in'], lw['qg'], lw['kg'], shared['hgrn_lb'], shared['e256']]
    assert len(args) == N_INPROJ_IN
    body = functools.partial(_inproj_kernel, layer, kv_transposed)
    aliases = {}
    if kv_prev is not None:
        in_specs += [pl.BlockSpec(memory_space=pl.ANY)] * 2
        args += list(kv_prev)
        aliases = {N_INPROJ_IN: 1, N_INPROJ_IN + 1: 2}
        body = functools.partial(_inproj_kernel_stacked, layer)
    return pl.pallas_call(
        body,
        grid=(m // tm,),
        in_specs=in_specs,
        out_specs=out_specs,
        out_shape=out_shape,
        input_output_aliases=aliases,
        compiler_params=_params("arbitrary"),
        name="inproj",
    )(*args)


def _outproj_kernel(x_ref, oa_ref, ob_ref, oc_ref, wa_ref, wb_ref, wc_ref, y_ref):
    y_ref[...] = (x_ref[...] + _dot(oa_ref[...], wa_ref[...]) + _dot(ob_ref[...], wb_ref[...])
                  + _dot(oc_ref[...], wc_ref[...]))


def _outproj(x2d, oa, ob, oc, lw, tm):
    m, d = x2d.shape
    row = lambda w: pl.BlockSpec((tm, w), lambda i: (i, 0))
    return pl.pallas_call(
        _outproj_kernel,
        grid=(m // tm,),
        in_specs=[row(d), row(D_A), row(D_B), row(D_C), lw.spec('wo_a'), lw.spec('wo_b'), lw.spec('wo_c')],
        out_specs=row(d),
        out_shape=jax.ShapeDtypeStruct((m, d), F32),
        compiler_params=_params("arbitrary"),
        name="outproj",
    )(x2d, oa, ob, oc, lw['wo_a'], lw['wo_b'], lw['wo_c'])


HEADS_PER_STEP = 8


def _moba_prompt_kernel(nb, slopes_ref, q_ref, kt_ref, vt_ref, g_ref, o_ref, kmt_ref, sel_ref, sd_ref, sdo_ref,
                        raw_ref):
    g = pl.program_id(1)
    i = pl.program_id(2)
    blk = MOBA_BLOCK
    nh = HEADS_PER_STEP
    width = nh * HEAD_DIM

    @pl.when(i == 0)
    def _():
        lane = lax.broadcasted_iota(jnp.int32, (width, LANES), 1)
        km = jnp.zeros((width, LANES), F32)
        for n in range(nb):
            col = jnp.sum(kt_ref[:, n * blk:(n + 1) * blk], axis=1, keepdims=True) * (1.0 / blk)
            km = jnp.where(lane == n, col, km)
        km_rows = jnp.concatenate([km.T[0:nb]] * nh, axis=0)
        r_head = lax.broadcasted_iota(jnp.int32, (nh * nb, width), 0) // nb
        c_head = lax.broadcasted_iota(jnp.int32, (nh * nb, width), 1) // HEAD_DIM
        km_hi, km_lo = _split2(jnp.where(r_head == c_head, km_rows, 0.0))
        kmt_ref[0] = km_hi.astype(BF16)
        kmt_ref[1] = km_lo.astype(BF16)
        tq = lax.broadcasted_iota(jnp.int32, (blk, blk), 1)
        tk = lax.broadcasted_iota(jnp.int32, (blk, blk), 0)
        d0 = (tq - tk).astype(F32)
        for h in range(nh):
            sd = (slopes_ref[nh * g + h] * LOG2E) * d0
            sd_ref[h] = sd
            sdo_ref[h] = jnp.where(d0 >= 0.0, sd, -NEG)

    qt = q_ref[...].T
    qts = _bf(qt * (QK_SCALE * LOG2E))

    def scores(h, start):
        r0 = h * HEAD_DIM
        kb = kt_ref[r0:r0 + HEAD_DIM, pl.ds(start, blk)]
        return _dot_tn(kb, qts[r0:r0 + HEAD_DIM])

    q_hi, q_lo = _split2(qt)
    km_hi = kmt_ref[0]
    gates = (jnp.dot(km_hi, q_hi, preferred_element_type=F32) + jnp.dot(kmt_ref[1], q_hi, preferred_element_type=F32)
             + jnp.dot(km_hi, q_lo, preferred_element_type=F32))
    for h in range(nh):
        sel_ref[h] = _topk_rows(gates[h * nb:(h + 1) * nb], nb, i)

    def head_step(h, carry, start, sp, shift, keep=None):
        m_run, l_run, acc = carry
        r0 = h * HEAD_DIM
        m_blk = jnp.max(sp, axis=0, keepdims=True) - shift
        off = shift
        if keep is not None:
            m_blk = jnp.where(keep > 0.5, m_blk, NEG)
            off = jnp.where(keep > 0.5, shift, -4.0 * NEG)
        m_new = jnp.maximum(m_run, m_blk)
        alpha = jnp.exp2(m_run - m_new)
        pt = jnp.exp2(sp - (m_new + off))
        l_new = alpha * l_run + jnp.sum(pt, axis=0, keepdims=True)
        vb = vt_ref[r0:r0 + HEAD_DIM, pl.ds(start, blk)]
        acc_new = alpha * acc + _dot(vb, pt)
        return m_new, l_new, acc_new

    def past(n, carries):
        start = pl.multiple_of(n * blk, blk)
        nxt = pl.multiple_of((n + 1) * blk, blk)
        gap = ((i - n) * blk).astype(F32) * LOG2E
        out = []
        for h in range(nh):
            sp = raw_ref[h] - sd_ref[h]
            raw_ref[h] = scores(h, nxt)
            out.append(head_step(h, carries[h], start, sp, slopes_ref[nh * g + h] * gap,
                                 keep=sel_ref[h, pl.ds(n, 1), :]))
        return tuple(out)

    init = tuple((jnp.full((1, blk), NEG, F32), jnp.zeros((1, blk), F32), jnp.zeros((HEAD_DIM, blk), F32))
                 for _ in range(nh))
    for h in range(nh):
        raw_ref[h] = scores(h, 0)
    carries = lax.fori_loop(0, i, past, init)
    start = pl.multiple_of(i * blk, blk)
    outs = []
    for h in range(nh):
        _, l_fin, acc = head_step(h, carries[h], start, raw_ref[h] - sdo_ref[h], 0.0)
        outs.append(acc / l_fin)
    o_ref[...] = jnp.concatenate(outs, axis=0).T * g_ref[...]


def _moba_prompt(q, kt, vt, gate, slopes, layer):
    b, t, _ = q.shape
    nb = t // MOBA_BLOCK
    width = HEADS_PER_STEP * HEAD_DIM
    qspec = pl.BlockSpec((None, MOBA_BLOCK, width), lambda bi, g, i: (bi, i, g))
    kspec = pl.BlockSpec((None, None, width, t), lambda bi, g, i: (bi, layer, g, 0))
    return pl.pallas_call(
        functools.partial(_moba_prompt_kernel, nb),
        grid=(b, N_HEADS_A // HEADS_PER_STEP, nb),
        in_specs=[pl.BlockSpec(memory_space=pltpu.SMEM), qspec, kspec, kspec, qspec],
        out_specs=qspec,
        out_shape=jax.ShapeDtypeStruct(q.shape, F32),
        scratch_shapes=[pltpu.VMEM((2, HEADS_PER_STEP * nb, width), BF16),
                        pltpu.VMEM((HEADS_PER_STEP, nb, MOBA_BLOCK), F32),
                        pltpu.VMEM((HEADS_PER_STEP, MOBA_BLOCK, MOBA_BLOCK), F32),
                        pltpu.VMEM((HEADS_PER_STEP, MOBA_BLOCK, MOBA_BLOCK), F32),
                        pltpu.VMEM((HEADS_PER_STEP, MOBA_BLOCK, MOBA_BLOCK), F32)],
        compiler_params=_params("arbitrary", "arbitrary", "arbitrary"),
        name="moba_prompt",
    )(slopes, q, kt, vt, gate)


SCORE_ROWS = 64
SEQS_PER_STEP = 2


def _moba_sample_kernel(n_pages, page, t_new, pt_ref, q_ref, kn_ref, vn_ref, g_ref, rowc_ref, hm_ref, *rest):
    n_ops = SEQS_PER_STEP * n_pages
    kp_refs = rest[:n_ops]
    vp_refs = rest[n_ops:2 * n_ops]
    o_ref = rest[2 * n_ops]
    s_ref = rest[2 * n_ops + 1]
    del pt_ref
    stages = [_moba_sample_seq(n_pages, page, t_new, s, q_ref, kn_ref, vn_ref, g_ref, rowc_ref, hm_ref,
                               kp_refs[s * n_pages:(s + 1) * n_pages], vp_refs[s * n_pages:(s + 1) * n_pages],
                               o_ref, s_ref)
              for s in range(SEQS_PER_STEP)]
    while stages:
        for gen in list(stages):
            if next(gen, StopIteration) is StopIteration:
                stages.remove(gen)


def _moba_sample_seq(n_pages, page, t_new, slot, q_ref, kn_ref, vn_ref, g_ref, rowc_ref, hm_ref, kp_refs, vp_refs,
                     o_ref, s_ref):
    tok = slice(slot * t_new, (slot + 1) * t_new)
    s_ref = s_ref.at[slot]
    nh = N_HEADS_A
    rows = SCORE_ROWS
    pages_per_blk = MOBA_BLOCK // page
    nblk = n_pages // pages_per_blk
    hm = hm_ref[...]
    q = q_ref[tok, :]
    qrep = jnp.concatenate([q] * (rows // t_new), axis=0) * hm
    qsb = _bf(qrep * QK_SCALE)
    slope = rowc_ref[:, 0:1]
    qpos = rowc_ref[:, 1:2]
    lane_f = lax.broadcasted_iota(jnp.int32, (rows, page), 1).astype(F32)

    lane_g = lax.broadcasted_iota(jnp.int32, (rows, LANES), 1)
    gates = jnp.zeros((rows, LANES), F32)
    for j in range(n_pages):
        kp = kp_refs[j][...].reshape(nh * HEAD_DIM, page)
        st = jnp.dot(qsb, _bf(kp), preferred_element_type=F32)
        gates = gates + jnp.where(lane_g == j // pages_per_blk, jnp.sum(st, axis=1, keepdims=True), 0.0)
        dist = (qpos - float(j * page)) - lane_f
        s_ref[:, j * page:(j + 1) * page] = st - slope * dist
    yield

    cnt = jnp.zeros((rows, LANES), jnp.int32)
    for m in range(nblk):
        col = gates[:, m:m + 1]
        cnt = cnt + ((col > gates) | ((col == gates) & (m < lane_g))).astype(jnp.int32)
    sel_t = (cnt < MOBA_TOPK).astype(F32)

    s_own = _dot_nt(qsb, kn_ref[tok, :])
    trow = lax.broadcasted_iota(jnp.int32, (rows, t_new), 0) % t_new
    tcol = lax.broadcasted_iota(jnp.int32, (rows, t_new), 1)
    dist_own = (trow - tcol).astype(F32)
    s_own = jnp.where(dist_own >= 0.0, s_own - slope * dist_own, NEG)
    yield

    mvec = jnp.full((rows, page), NEG, F32)
    for j in range(n_pages):
        n = j // pages_per_blk
        st = jnp.where(sel_t[:, n:n + 1] > 0.5, s_ref[:, j * page:(j + 1) * page], NEG)
        s_ref[:, j * page:(j + 1) * page] = st
        mvec = jnp.maximum(mvec, st)
    m_row = jnp.maximum(jnp.max(mvec, axis=1, keepdims=True), jnp.max(s_own, axis=1, keepdims=True))
    yield

    p_own = jnp.exp(s_own - m_row)
    lvec = jnp.zeros((rows, page), F32)
    acc = _dot(p_own, vn_ref[tok, :])
    for j in range(n_pages):
        pj = jnp.exp(s_ref[:, j * page:(j + 1) * page] - m_row)
        lvec = lvec + pj
        acc = acc + _dot_nt(pj, vp_refs[j][...].reshape(nh * HEAD_DIM, page))
    yield
    l_row = jnp.sum(lvec, axis=1, keepdims=True) + jnp.sum(p_own, axis=1, keepdims=True)
    tot = acc * (1.0 / l_row) * hm
    out = tot[0:t_new]
    for h in range(1, nh):
        out = out + tot[h * t_new:(h + 1) * t_new]
    o_ref[tok, :] = out * g_ref[tok, :]


def _moba_sample(q, k_new, v_new, gate, cache_kt, cache_vt, pt_flat, layer, consts, n_seq, t_new, n_pages):
    page = cache_kt.shape[-1]
    rowc, hm = consts
    per = SEQS_PER_STEP
    tspec = pl.BlockSpec((per * t_new, D_A), lambda b, pt: (b, 0))

    def page_spec(s, j):
        return pl.BlockSpec((None, None, N_HEADS_A, HEAD_DIM, page),
                            lambda b, pt, s=s, j=j: (pt[(b * per + s) * n_pages + j], layer, 0, 0, 0))

    cspec = lambda a: pl.BlockSpec(a.shape, lambda b, pt: (0,) * a.ndim)
    pages = [page_spec(s, j) for s in range(per) for j in range(n_pages)]
    grid_spec = pltpu.PrefetchScalarGridSpec(
        num_scalar_prefetch=1,
        grid=(n_seq // per,),
        in_specs=[tspec, tspec, tspec, tspec, cspec(rowc), cspec(hm)] + pages + pages,
        out_specs=tspec,
        scratch_shapes=[pltpu.VMEM((per, SCORE_ROWS, n_pages * page), F32)],
    )
    return pl.pallas_call(
        functools.partial(_moba_sample_kernel, n_pages, page, t_new),
        grid_spec=grid_spec,
        out_shape=jax.ShapeDtypeStruct(q.shape, F32),
        compiler_params=_params("arbitrary"),
        name="moba_sample",
    )(pt_flat, q, k_new, v_new, gate, rowc, hm, *([cache_kt] * (per * n_pages)), *([cache_vt] * (per * n_pages)))


def _moba_sample_consts(t_new, past_len):
    nh = N_HEADS_A
    used = nh * t_new
    r = np.arange(SCORE_ROWS)
    live = r < used
    slopes = 2.0 ** (-8.0 * (np.arange(nh) + 1) / nh)
    rowc = np.zeros((SCORE_ROWS, 2), np.float32)
    rowc[:, 0] = np.where(live, slopes[np.minimum(r // t_new, nh - 1)], 0.0)
    rowc[:, 1] = past_len + (r % t_new)
    c = np.arange(nh * HEAD_DIM)
    hm = (((c[None, :] // HEAD_DIM) == (r[:, None] // t_new)) & live[:, None]).astype(np.float32)
    return jnp.asarray(rowc), jnp.asarray(hm)


def _hgrn_consts(c):
    levels = int(round(math.log2(c)))
    t = np.arange(c)[:, None]
    u = np.arange(c)[None, :]
    mats = [u <= t]
    masks = [t == u]
    for j in range(1, levels + 1):
        p = 2 ** j
        hlf = p // 2
        mid = (t // p) * p + hlf
        upper = (t % p) >= hlf
        mats.append((upper & (u >= mid) & (u <= t)) | ((~upper) & (u > t) & (u <= mid - 1)))
        masks.append(((t // p) == (u // p)) & upper & ((u % p) < hlf))
    w_all = jnp.asarray(np.concatenate(mats, axis=0), dtype=BF16)
    mk = jnp.asarray(np.stack(masks), dtype=F32)
    return w_all, mk


def _hgrn_kernel(c, levels, q_ref, lf_ref, kk_ref, v_ref, g_ref, ng_ref, w_ref, mk_ref, e_ref,
                 o_ref, sn_ref, sbd_ref):
    nh = N_HEADS_B
    width = nh * HEAD_DIM
    lf = lf_ref[...]
    d_all = _dot_sel(w_ref[...], lf, parts=2)
    b = d_all[0:c]
    e_b = jnp.exp(b)
    e_end = jnp.exp(b[c - 1:c, :] - b)
    q = q_ref[...]
    k = kk_ref[...]
    v = v_ref[...]
    vb = _bf(v)
    sbd = sbd_ref[...]
    o = _dot(q * e_b, sbd)
    yield
    lane = lax.broadcasted_iota(jnp.int32, (c, width), 1)
    qk_levels = [(q, k)]
    for j in range(1, levels + 1):
        e_j = jnp.exp(d_all[j * c:(j + 1) * c])
        qk_levels.append((q * e_j, k * e_j))
    heads = [(lane >= h * HEAD_DIM) & (lane < (h + 1) * HEAD_DIM) for h in range(nh)]
    kbs = [_bf(kj) for _, kj in qk_levels]
    a_heads = []
    for h in range(nh):
        a = jnp.zeros((c, c), F32)
        for j, (qj, _) in enumerate(qk_levels):
            a = a + _dot_nt(jnp.where(heads[h], qj, 0.0), kbs[j]) * mk_ref[j]
        a_heads.append(_bf(a))
        yield
    for h in range(nh):
        o = o + jnp.where(heads[h], jnp.dot(a_heads[h], vb, preferred_element_type=F32), 0.0)

    yield
    b_end_col = jnp.broadcast_to(b[c - 1:c, :], (SUBLANES, width)).T[:, 0:1]
    decay = jnp.exp(b_end_col)
    r = lax.broadcasted_iota(jnp.int32, (width, width), 0) // HEAD_DIM
    cc = lax.broadcasted_iota(jnp.int32, (width, width), 1) // HEAD_DIM
    s_new = sbd * decay + jnp.where(r == cc, _dot_tn(k * e_end, v), 0.0)
    sbd_ref[...] = s_new
    yield

    ss = _seg_sum(o * o, e_ref)
    o_ref[...] = o * lax.rsqrt(ss * (1.0 / HEAD_DIM) + EPS) * ng_ref[...] * g_ref[...]
    for h in range(nh):
        sn_ref[h] = s_new[h * HEAD_DIM:(h + 1) * HEAD_DIM, h * HEAD_DIM:(h + 1) * HEAD_DIM]


def _mlstm_consts(c):
    t = np.arange(c)[:, None]
    u = np.arange(c)[None, :]
    return jnp.asarray(u <= t, dtype=BF16)


def _mlstm_kernel(c, uc_ref, g_ref, cw_ref, cb_ref, wq_ref, wk_ref, wv_ref,
                  wg_ref, bg_ref, skip_ref, ng_ref, tri_ref, e_ref,
                  o_ref, cn_ref, nn_ref, mn_ref, cvn_ref,
                  uext_ref, cbd_ref, n_ref, m_ref):
    nh = N_HEADS_C
    width = nh * HEAD_DIM

    uc = uc_ref[...]
    uext_ref[8:8 + c, :] = uc
    conv = cb_ref[...]
    for j in range(CONV_W):
        conv = conv + cw_ref[j:j + 1, :] * uext_ref[5 + j:5 + j + c, :]
    tail = uext_ref[c:c + 8, :]
    uext_ref[0:8, :] = tail
    uconv = _silu(conv)

    qm = _dot(uconv, wq_ref[...])
    km = _dot(uconv, wk_ref[...])
    vm = _dot(uc, wv_ref[...])
    i_raw = _dot3(qm, wg_ref[0]) + _dot3(km, wg_ref[1]) + _dot3(vm, wg_ref[2]) + bg_ref[...]
    logf = pltpu.roll(_log_sigmoid(i_raw), GATE_LANES - N_HEADS_C, 1)
    yield
    bcum = _dot_sel(tri_ref[...], logf)
    a = i_raw - bcum
    rowi = lax.broadcasted_iota(jnp.int32, (c, GATE_LANES), 0)
    s = 1
    while s < c:
        a = jnp.maximum(a, jnp.where(rowi >= s, pltpu.roll(a, s, 0), -jnp.inf))
        s *= 2
    m0 = m_ref[...]
    m_t = bcum + jnp.maximum(m0, a)
    g_in = jnp.exp(bcum + m0 - m_t)
    bm = bcum - m_t
    ib_t = (i_raw - bcum).T
    m_end = m_t[c - 1:c, :]
    b_end = bcum[c - 1:c, :]
    w_tok = jnp.exp((b_end - bcum) + i_raw - m_end)
    g_end = jnp.exp(b_end + m0 - m_end)

    yield
    ks = km * QK_SCALE
    ksb = _bf(ks)
    vmb = _bf(vm)
    lane = lax.broadcasted_iota(jnp.int32, (c, width), 1)
    trow = lax.broadcasted_iota(jnp.int32, (c, c), 0)
    tcol = lax.broadcasted_iota(jnp.int32, (c, c), 1)
    causal = tcol <= trow
    g256 = _spread_heads(g_in, c, nh)
    cbd = cbd_ref[...]
    n0 = n_ref[...]
    num = g256 * _dot(qm, cbd)
    qn = qm * n0
    den_cols = jnp.zeros((c, GATE_LANES), F32)
    lane_g = lax.broadcasted_iota(jnp.int32, (c, GATE_LANES), 1)
    heads = [(lane >= h * HEAD_DIM) & (lane < (h + 1) * HEAD_DIM) for h in range(nh)]
    raw = [_dot_nt(jnp.where(heads[h], qm, 0.0), ksb) for h in range(nh)]
    for h in range(nh):
        expo = bm[:, h:h + 1] + ib_t[h:h + 1, :]
        dmat = jnp.exp(jnp.where(causal, expo, NEG))
        qk = raw[h] * dmat
        num = num + jnp.where(heads[h], jnp.dot(_bf(qk), vmb, preferred_element_type=F32), 0.0)
        den_h = (g_in[:, h:h + 1] * jnp.sum(jnp.where(heads[h], qn, 0.0), axis=1, keepdims=True)
                 + jnp.sum(qk, axis=1, keepdims=True))
        den_cols = jnp.where(lane_g == h, den_h, den_cols)
        yield
    denom = jnp.maximum(jnp.abs(den_cols), jnp.exp(-m_t))
    hval = num / _spread_heads(denom, c, nh)

    w256 = _spread_heads(w_tok, c, nh)
    gend256 = _spread_heads(g_end, 1, nh)
    r = lax.broadcasted_iota(jnp.int32, (width, width), 0) // HEAD_DIM
    cc = lax.broadcasted_iota(jnp.int32, (width, width), 1) // HEAD_DIM
    kw = ks * w256
    c_new = cbd * gend256 + jnp.where(r == cc, _dot_tn(kw, vm), 0.0)
    n_new = gend256 * n0 + jnp.sum(kw, axis=0, keepdims=True)
    cbd_ref[...] = c_new
    n_ref[...] = n_new
    m_ref[...] = m_end
    yield

    mean = _seg_sum(hval, e_ref) * (1.0 / HEAD_DIM)
    xc = hval - mean
    var = _seg_sum(xc * xc, e_ref) * (1.0 / HEAD_DIM)
    hc = xc * lax.rsqrt(var + EPS) * ng_ref[...]
    o_ref[...] = (hc + skip_ref[...] * uconv) * g_ref[...]
    for h in range(nh):
        cn_ref[h] = c_new[h * HEAD_DIM:(h + 1) * HEAD_DIM, h * HEAD_DIM:(h + 1) * HEAD_DIM]
    nn_ref[...] = n_new
    mn_ref[...] = m_end
    cvn_ref[...] = tail


N_HGRN_IN, N_HGRN_OUT, N_HGRN_SCRATCH = 9, 2, 1
N_MLSTM_IN, N_MLSTM_OUT, N_MLSTM_SCRATCH = 13, 5, 4


def _recur_kernel(c, levels, *refs):
    i0 = 0
    hg_in = refs[i0:i0 + N_HGRN_IN]
    i0 += N_HGRN_IN
    ml_in = refs[i0:i0 + N_MLSTM_IN]
    i0 += N_MLSTM_IN
    hg_out = refs[i0:i0 + N_HGRN_OUT]
    i0 += N_HGRN_OUT
    ml_out = refs[i0:i0 + N_MLSTM_OUT]
    i0 += N_MLSTM_OUT
    hg_scr = refs[i0:i0 + N_HGRN_SCRATCH]
    i0 += N_HGRN_SCRATCH
    ml_scr = refs[i0:i0 + N_MLSTM_SCRATCH]

    @pl.when(pl.program_id(1) == 0)
    def _():
        for ref in hg_scr + ml_scr:
            ref[...] = jnp.zeros(ref.shape, F32)

    stages = [_hgrn_kernel(c, levels, *hg_in, *hg_out, *hg_scr), _mlstm_kernel(c, *ml_in, *ml_out, *ml_scr)]
    while stages:
        for gen in list(stages):
            if next(gen, StopIteration) is StopIteration:
                stages.remove(gen)


def _recur_prompt(qb, lf, kk, ib, gb, uc, gc, lw, shared, c):
    b, t, width = qb.shape
    w_all, mk = shared['hgrn_consts']
    levels = mk.shape[0] - 1
    tok = pl.BlockSpec((None, c, width), lambda bi, ci: (bi, ci, 0))
    per_b = lambda shp: pl.BlockSpec((None,) + shp, lambda bi, ci: (bi,) + (0,) * len(shp))
    ml_names = ['conv_w', 'conv_b', 'wq', 'wk', 'wv', 'wg', 'bg', 'skip', 'mng']
    hg_w = [lw['hng'], w_all, mk, shared['e256']]
    ml_w = [lw[n] for n in ml_names] + [shared['mlstm_tri'], shared['e256']]
    hg_specs = [lw.spec('hng')] + [_const_spec(w.shape) for w in hg_w[1:]]
    ml_specs = [lw.spec(n) for n in ml_names] + [_const_spec(w.shape) for w in ml_w[len(ml_names):]]
    assert 5 + len(hg_w) == N_HGRN_IN and 2 + len(ml_w) == N_MLSTM_IN
    state = (N_HEADS_B, HEAD_DIM, HEAD_DIM)
    out_shape = [jax.ShapeDtypeStruct(qb.shape, F32), jax.ShapeDtypeStruct((b,) + state, F32),
                 jax.ShapeDtypeStruct(uc.shape, F32), jax.ShapeDtypeStruct((b,) + state, F32),
                 jax.ShapeDtypeStruct((b, 1, width), F32), jax.ShapeDtypeStruct((b, 1, GATE_LANES), F32),
                 jax.ShapeDtypeStruct((b, SUBLANES, width), F32)]
    return pl.pallas_call(
        functools.partial(_recur_kernel, c, levels),
        grid=(b, t // c),
        in_specs=[tok] * 5 + hg_specs + [tok] * 2 + ml_specs,
        out_specs=[tok, per_b(state), tok, per_b(state), per_b((1, width)), per_b((1, GATE_LANES)),
                   per_b((SUBLANES, width))],
        out_shape=out_shape,
        scratch_shapes=[pltpu.VMEM((width, width), F32),
                        pltpu.VMEM((c + SUBLANES, width), F32), pltpu.VMEM((width, width), F32),
                        pltpu.VMEM((1, width), F32), pltpu.VMEM((1, GATE_LANES), F32)],
        compiler_params=_params("arbitrary", "arbitrary"),
        name="recur_prompt",
    )(qb, lf, kk, ib, gb, *hg_w, uc, gc, *ml_w)


def _recur_attn_kernel(c, levels, n_pages, page, t_new, pt_ref, *refs):
    del pt_ref
    n_ops = SEQS_PER_STEP * n_pages
    n_rec_in = N_HGRN_IN + N_MLSTM_IN
    n_att_in = 6 + 2 * n_ops
    n_rec_out = N_HGRN_OUT + N_MLSTM_OUT
    rec_in = refs[:n_rec_in]
    att_in = refs[n_rec_in:n_rec_in + n_att_in]
    outs = refs[n_rec_in + n_att_in:n_rec_in + n_att_in + n_rec_out + 1]
    scr = refs[n_rec_in + n_att_in + n_rec_out + 1:]
    hg_in, ml_in = rec_in[:N_HGRN_IN], rec_in[N_HGRN_IN:]
    hg_out, ml_out, o_ref = outs[:N_HGRN_OUT], outs[N_HGRN_OUT:n_rec_out], outs[n_rec_out]
    hg_scr, ml_scr = scr[:N_HGRN_SCRATCH], scr[N_HGRN_SCRATCH:N_HGRN_SCRATCH + N_MLSTM_SCRATCH]
    s_ref = scr[N_HGRN_SCRATCH + N_MLSTM_SCRATCH]
    q_ref, kn_ref, vn_ref, g_ref, rowc_ref, hm_ref = att_in[:6]
    kp_refs, vp_refs = att_in[6:6 + n_ops], att_in[6 + n_ops:]

    @pl.when(pl.program_id(1) == 0)
    def _():
        for ref in hg_scr + ml_scr:
            ref[...] = jnp.zeros(ref.shape, F32)

    seqs = [_moba_sample_seq(n_pages, page, t_new, s, q_ref, kn_ref, vn_ref, g_ref, rowc_ref, hm_ref,
                             kp_refs[s * n_pages:(s + 1) * n_pages], vp_refs[s * n_pages:(s + 1) * n_pages],
                             o_ref, s_ref)
            for s in range(SEQS_PER_STEP)]
    stages = [_hgrn_kernel(c, levels, *hg_in, *hg_out, *hg_scr), seqs[0],
              _mlstm_kernel(c, *ml_in, *ml_out, *ml_scr)] + seqs[1:]
    while stages:
        for gen in list(stages):
            if next(gen, StopIteration) is StopIteration:
                stages.remove(gen)


def _recur_prompt_with_sample_attention(qb, lf, kk, ib, gb, uc, gc, lw, shared, c,
                                        q, k_new, v_new, gate, cache_kt, cache_vt, pt_flat, n_seq, t_new, n_pages):
    b, t, width = qb.shape
    nc = t // c
    per = SEQS_PER_STEP
    assert b * nc * per == n_seq
    layer = lw.layer
    page = cache_kt.shape[-1]
    rowc, hm = shared['sample_consts']
    w_all, mk = shared['hgrn_consts']
    levels = mk.shape[0] - 1
    tok = pl.BlockSpec((None, c, width), lambda bi, ci, pt: (bi, ci, 0))
    per_b = lambda shp: pl.BlockSpec((None,) + shp, lambda bi, ci, pt: (bi,) + (0,) * len(shp))
    ml_names = ['conv_w', 'conv_b', 'wq', 'wk', 'wv', 'wg', 'bg', 'skip', 'mng']
    hg_w = [lw['hng'], w_all, mk, shared['e256']]
    ml_w = [lw[n] for n in ml_names] + [shared['mlstm_tri'], shared['e256']]
    hg_specs = [lw.spec('hng')] + [_const_spec(w.shape) for w in hg_w[1:]]
    ml_specs = [lw.spec(n) for n in ml_names] + [_const_spec(w.shape) for w in ml_w[len(ml_names):]]
    tspec = pl.BlockSpec((per * t_new, D_A), lambda bi, ci, pt: (bi * nc + ci, 0))

    def page_spec(s, j):
        return pl.BlockSpec((None, None, N_HEADS_A, HEAD_DIM, page),
                            lambda bi, ci, pt, s=s, j=j: (pt[((bi * nc + ci) * per + s) * n_pages + j], layer, 0, 0, 0))

    pages = [page_spec(s, j) for s in range(per) for j in range(n_pages)]
    state = (N_HEADS_B, HEAD_DIM, HEAD_DIM)
    out_shape = [jax.ShapeDtypeStruct(qb.shape, F32), jax.ShapeDtypeStruct((b,) + state, F32),
                 jax.ShapeDtypeStruct(uc.shape, F32), jax.ShapeDtypeStruct((b,) + state, F32),
                 jax.ShapeDtypeStruct((b, 1, width), F32), jax.ShapeDtypeStruct((b, 1, GATE_LANES), F32),
                 jax.ShapeDtypeStruct((b, SUBLANES, width), F32), jax.ShapeDtypeStruct(q.shape, F32)]
    grid_spec = pltpu.PrefetchScalarGridSpec(
        num_scalar_prefetch=1,
        grid=(b, nc),
        in_specs=[tok] * 5 + hg_specs + [tok] * 2 + ml_specs
        + [tspec] * 4 + [_const_spec(rowc.shape), _const_spec(hm.shape)] + pages + pages,
        out_specs=[tok, per_b(state), tok, per_b(state), per_b((1, width)), per_b((1, GATE_LANES)),
                   per_b((SUBLANES, width)), tspec],
        scratch_shapes=[pltpu.VMEM((width, width), F32),
                        pltpu.VMEM((c + SUBLANES, width), F32), pltpu.VMEM((width, width), F32),
                        pltpu.VMEM((1, width), F32), pltpu.VMEM((1, GATE_LANES), F32),
                        pltpu.VMEM((per, SCORE_ROWS, n_pages * page), F32)],
    )
    return pl.pallas_call(
        functools.partial(_recur_attn_kernel, c, levels, n_pages, page, t_new),
        grid_spec=grid_spec,
        out_shape=out_shape,
        compiler_params=_params("arbitrary", "arbitrary"),
        name="recur_prompt_attn_sample",
    )(pt_flat, qb, lf, kk, ib, gb, *hg_w, uc, gc, *ml_w, q, k_new, v_new, gate, rowc, hm,
      *([cache_kt] * (per * n_pages)), *([cache_vt] * (per * n_pages)))


K_UNROLL = 8


def _hgrn_sample_kernel(t_new, q_ref, kk_ref, v_ref, g_ref, s0_ref, ng_ref, o_ref, sn_ref):
    sn_ref[...] = s0_ref[...]
    for t in range(t_new):
        vt = v_ref[t]

        def kbody(kb, o, t=t):
            for kk in range(K_UNROLL):
                k = kb * K_UNROLL + kk
                kt = kk_ref[t, pl.ds(k, 1), :]
                s_k = (1.0 - kt) * sn_ref[k] + kt * vt
                sn_ref[k] = s_k
                o = o + s_k * q_ref[t, pl.ds(k, 1), :]
            return o

        o = lax.fori_loop(0, HEAD_DIM // K_UNROLL, kbody, jnp.zeros(vt.shape, F32))
        ss = jnp.sum(o * o, axis=0, keepdims=True) * (1.0 / HEAD_DIM)
        o_ref[t] = o * lax.rsqrt(ss + EPS) * ng_ref[...] * g_ref[t]


def _hgrn_sample(qt, kkt, vt, gt, s_all, lw):
    t_new, width, b = qt.shape
    layer = lw.layer
    tok = pl.BlockSpec((t_new, HEAD_DIM, b), lambda h: (0, h, 0))
    st_in = pl.BlockSpec((None, None, HEAD_DIM, HEAD_DIM, b), lambda h: (layer, h, 0, 0, 0))
    st = pl.BlockSpec((None, HEAD_DIM, HEAD_DIM, b), lambda h: (h, 0, 0, 0))
    return pl.pallas_call(
        functools.partial(_hgrn_sample_kernel, t_new),
        grid=(N_HEADS_B,),
        in_specs=[tok, tok, tok, tok, st_in, lw.head_col_spec('hng_col')],
        out_specs=[tok, st],
        out_shape=[jax.ShapeDtypeStruct(qt.shape, F32), jax.ShapeDtypeStruct(s_all.shape[1:], F32)],
        compiler_params=_params("arbitrary"),
        name="hgrn_sample",
    )(qt, kkt, vt, gt, s_all, lw['hng_col'])


def _mlstm_front_kernel(t_new, uc_ref, cv0_ref, cw_ref, cb_ref, wq_ref, wk_ref, wv_ref, wi_ref, wf_ref,
                        bi_ref, bf_ref, uconv_ref, qm_ref, km_ref, vm_ref, i_ref, lf_ref):
    hist = [cv0_ref[j] for j in range(CONV_W - 1)] + [uc_ref[t] for t in range(t_new)]
    for t in range(t_new):
        conv = cb_ref[...]
        for j in range(CONV_W):
            conv = conv + cw_ref[j] * hist[t + j]
        uconv = _silu(conv)
        uconv_ref[t] = uconv
        qm = _dot(wq_ref[...], uconv)
        km = _dot(wk_ref[...], uconv)
        vm = _dot(wv_ref[...], hist[t + CONV_W - 1])
        qm_ref[t] = qm
        km_ref[t] = km
        vm_ref[t] = vm
        i_ref[t] = _dot3(wi_ref[0], qm) + _dot3(wi_ref[1], km) + _dot3(wi_ref[2], vm) + bi_ref[...]
        lf_ref[t] = _log_sigmoid(_dot3(wf_ref[0], qm) + _dot3(wf_ref[1], km) + _dot3(wf_ref[2], vm)
                                 + bf_ref[...])


def _mlstm_front(uct, cv_all, lw):
    t_new, width, b = uct.shape
    layer = lw.layer
    names = ['conv_w_col', 'conv_b_col', 'wq_t', 'wk_t', 'wv_t', 'wi_t', 'wf_t', 'bi_col', 'bf_col']
    big = jax.ShapeDtypeStruct(uct.shape, F32)
    small = jax.ShapeDtypeStruct((t_new, SUBLANES, b), F32)
    cv_spec = pl.BlockSpec((None,) + cv_all.shape[1:], lambda i: (layer, 0, 0, 0))
    return pl.pallas_call(
        functools.partial(_mlstm_front_kernel, t_new),
        grid=(1,),
        in_specs=[_const_spec(uct.shape), cv_spec] + [lw.spec(n) for n in names],
        out_specs=[_const_spec(uct.shape)] * 4 + [_const_spec(small.shape)] * 2,
        out_shape=[big] * 4 + [small] * 2,
        compiler_params=_params("arbitrary"),
        name="mlstm_front",
    )(uct, cv_all, *[lw[n] for n in names])


def _mlstm_sample_kernel(t_new, qm_ref, km_ref, vm_ref, i_ref, lf_ref, uconv_ref, g_ref, c0_ref, n0_ref, m0_ref,
                         skip_ref, ng_ref, o_ref, cn_ref, nn_ref, mn_ref):
    h = pl.program_id(0)
    cn_ref[...] = c0_ref[...]
    n = n0_ref[...]
    m = m0_ref[...]
    for t in range(t_new):
        i_t = i_ref[t, pl.ds(h, 1), :]
        lf_t = lf_ref[t, pl.ds(h, 1), :]
        m_new = jnp.maximum(lf_t + m, i_t)
        fp = jnp.exp(lf_t + m - m_new)
        ip = jnp.exp(i_t - m_new)
        vt = vm_ref[t]
        n = fp * n + ip * (km_ref[t] * QK_SCALE)

        def kbody(kb, num, t=t, fp=fp, ip=ip, vt=vt):
            for kk in range(K_UNROLL):
                k = kb * K_UNROLL + kk
                kt = km_ref[t, pl.ds(k, 1), :] * QK_SCALE
                c_k = fp * cn_ref[k] + (ip * kt) * vt
                cn_ref[k] = c_k
                num = num + c_k * qm_ref[t, pl.ds(k, 1), :]
            return num

        num = lax.fori_loop(0, HEAD_DIM // K_UNROLL, kbody, jnp.zeros(vt.shape, F32))
        den = jnp.sum(qm_ref[t] * n, axis=0, keepdims=True)
        hval = num / jnp.maximum(jnp.abs(den), jnp.exp(-m_new))
        m = m_new
        mean = jnp.sum(hval, axis=0, keepdims=True) * (1.0 / HEAD_DIM)
        xc = hval - mean
        var = jnp.sum(xc * xc, axis=0, keepdims=True) * (1.0 / HEAD_DIM)
        hc = xc * lax.rsqrt(var + EPS) * ng_ref[...]
        o_ref[t] = (hc + skip_ref[...] * uconv_ref[t]) * g_ref[t]
    nn_ref[...] = n
    mn_ref[...] = m


def _mlstm_sample(front, gt, c_all, n_all, m_all, lw):
    uconv, qm, km, vm, i_raw, logf = front
    t_new, width, b = qm.shape
    layer = lw.layer
    tok = pl.BlockSpec((t_new, HEAD_DIM, b), lambda h: (0, h, 0))
    gates = _const_spec(i_raw.shape)
    st = pl.BlockSpec((None, HEAD_DIM, HEAD_DIM, b), lambda h: (h, 0, 0, 0))
    nst = pl.BlockSpec((None, HEAD_DIM, b), lambda h: (h, 0, 0))
    mst = pl.BlockSpec((None, 1, b), lambda h: (h, 0, 0))
    st_in = pl.BlockSpec((None, None, HEAD_DIM, HEAD_DIM, b), lambda h: (layer, h, 0, 0, 0))
    nst_in = pl.BlockSpec((None, None, HEAD_DIM, b), lambda h: (layer, h, 0, 0))
    mst_in = pl.BlockSpec((None, None, 1, b), lambda h: (layer, h, 0, 0))
    return pl.pallas_call(
        functools.partial(_mlstm_sample_kernel, t_new),
        grid=(N_HEADS_C,),
        in_specs=[tok, tok, tok, gates, gates, tok, tok, st_in, nst_in, mst_in,
                  lw.head_col_spec('skip_col'), lw.head_col_spec('mng_col')],
        out_specs=[tok, st, nst, mst],
        out_shape=[jax.ShapeDtypeStruct(qm.shape, F32), jax.ShapeDtypeStruct(c_all.shape[1:], F32),
                   jax.ShapeDtypeStruct(n_all.shape[1:], F32), jax.ShapeDtypeStruct(m_all.shape[1:], F32)],
        compiler_params=_params("arbitrary"),
        name="mlstm_sample",
    )(qm, km, vm, i_raw, logf, uconv, gt, c_all, n_all, m_all, lw['skip_col'], lw['mng_col'])


class _LayerWeights:
    def __init__(self, stacked, layer):
        self.stacked = stacked
        self.layer = layer

    def __getitem__(self, name):
        return self.stacked[name]

    def spec(self, name):
        arr = self.stacked[name]
        layer = self.layer
        nd = arr.ndim - 1
        return pl.BlockSpec((None,) + arr.shape[1:], lambda *_: (layer,) + (0,) * nd)

    def head_col_spec(self, name):
        layer = self.layer
        return pl.BlockSpec((None, HEAD_DIM, 1), lambda h: (layer, h, 0))


def _stacked_weights(norm_g, w_in, q_norm_g, k_norm_g, hgrn_norm_g, mlstm_conv_w, mlstm_conv_b, mlstm_wq,
                     mlstm_wk, mlstm_wv, mlstm_w_ig, mlstm_b_ig, mlstm_w_fg, mlstm_b_fg, mlstm_skip, mlstm_norm_g,
                     w_out):
    depth = w_in.shape[0]
    eye = jnp.eye(N_HEADS_C, dtype=F32)

    def bd(w):
        return jnp.einsum('lhde,hg->lhdge', w, eye).reshape(depth, D_C, D_C).astype(BF16)

    def gate3(w):
        return w.reshape(depth, 3, D_C, N_HEADS_C)

    def gate_t(w):
        return jnp.pad(jnp.swapaxes(gate3(w), 2, 3), ((0, 0), (0, 0), (0, SUBLANES - N_HEADS_C), (0, 0)))

    def col8(v):
        return jnp.pad(v, ((0, 0), (0, SUBLANES - N_HEADS_C)))[:, :, None]

    wg = jnp.concatenate([gate3(mlstm_w_ig), gate3(mlstm_w_fg)], axis=-1)
    bg = jnp.concatenate([mlstm_b_ig, mlstm_b_fg], axis=-1)
    return {
        'norm_g': norm_g[:, None, :],
        'w_in': w_in.astype(BF16),
        'qg': jnp.tile(q_norm_g, (1, N_HEADS_A))[:, None, :],
        'kg': jnp.tile(k_norm_g, (1, N_HEADS_A))[:, None, :],
        'hng': hgrn_norm_g[:, None, :],
        'hng_col': hgrn_norm_g[:, :, None],
        'conv_w': mlstm_conv_w,
        'conv_b': mlstm_conv_b[:, None, :],
        'conv_w_col': mlstm_conv_w[:, :, :, None],
        'conv_b_col': mlstm_conv_b[:, :, None],
        'wq': bd(mlstm_wq), 'wk': bd(mlstm_wk), 'wv': bd(mlstm_wv),
        'wq_t': bd(jnp.swapaxes(mlstm_wq, 2, 3)), 'wk_t': bd(jnp.swapaxes(mlstm_wk, 2, 3)),
        'wv_t': bd(jnp.swapaxes(mlstm_wv, 2, 3)),
        'wg': jnp.pad(wg, ((0, 0), (0, 0), (0, 0), (0, GATE_LANES - 2 * N_HEADS_C))),
        'wi_t': gate_t(mlstm_w_ig), 'wf_t': gate_t(mlstm_w_fg),
        'bg': jnp.pad(bg, ((0, 0), (0, GATE_LANES - 2 * N_HEADS_C)))[:, None, :],
        'bi_col': col8(mlstm_b_ig), 'bf_col': col8(mlstm_b_fg),
        'skip': mlstm_skip[:, None, :], 'mng': mlstm_norm_g[:, None, :],
        'skip_col': mlstm_skip[:, :, None], 'mng_col': mlstm_norm_g[:, :, None],
        'wo_a': w_out[:, :D_A].astype(BF16),
        'wo_b': w_out[:, D_A:D_A + D_B].astype(BF16),
        'wo_c': w_out[:, D_A + D_B:].astype(BF16),
    }


def _layer(xp, xs, layer, lw, shared, chunk, tm_p, tm_s, depth, kv_prev, cache_kt, cache_vt, pt_flat,
           s_all, c_all, n_all, m_all, cv_all):
    b, t, d = xp.shape
    bs, ts, _ = xs.shape
    n_pages = pt_flat.shape[0] // bs
    xp2d = xp.reshape(b * t, d)
    xs2d = xs.reshape(bs * ts, d)
    (qa, kt, vt, ga, qb, lf, kk, ib, gb, uc, gc) = _inproj(xp2d, layer, lw, shared, tm_p, seq_len=t, depth=depth,
                                                           kv_prev=kv_prev)
    (qa_s, ka_s, va_s, ga_s, qb_s, _, kk_s, ib_s, gb_s, uc_s, gc_s) = _inproj(xs2d, layer, lw, shared, tm_s)
    r3 = lambda a: a.reshape(b, t, a.shape[-1])
    oa = _moba_prompt(r3(qa), kt, vt, r3(ga), shared['slopes'], layer)
    rec_args = (r3(qb), r3(lf), r3(kk), r3(ib), r3(gb), r3(uc), r3(gc), lw, shared, chunk)
    if b * (t // chunk) * SEQS_PER_STEP == bs:
        ob, s_new, oc, c_new, n_new, m_new, cv_new, oa_s = _recur_prompt_with_sample_attention(
            *rec_args, qa_s, ka_s, va_s, ga_s, cache_kt, cache_vt, pt_flat, bs, ts, n_pages)
    else:
        ob, s_new, oc, c_new, n_new, m_new, cv_new = _recur_prompt(*rec_args)
        oa_s = _moba_sample(qa_s, ka_s, va_s, ga_s, cache_kt, cache_vt, pt_flat, layer, shared['sample_consts'],
                            bs, ts, n_pages)
    yp = _outproj(xp2d, oa.reshape(b * t, D_A), ob.reshape(b * t, D_B), oc.reshape(b * t, D_C), lw, tm_p)
    res_p = (yp.reshape(b, t, d), kt, vt, s_new, c_new, n_new.reshape(b, N_HEADS_C, HEAD_DIM),
             m_new[:, 0, :N_HEADS_C], cv_new[:, SUBLANES - (CONV_W - 1):, :])

    to_lanes = lambda a: jnp.transpose(a.reshape(bs, ts, a.shape[-1]), (1, 2, 0))
    from_lanes = lambda a: jnp.transpose(a, (2, 0, 1)).reshape(bs * ts, a.shape[1])
    obt, s_new_s = _hgrn_sample(to_lanes(qb_s), to_lanes(kk_s), to_lanes(ib_s), to_lanes(gb_s), s_all, lw)
    front = _mlstm_front(to_lanes(uc_s), cv_all, lw)
    oct, c_new_s, n_new_s, m_new_s = _mlstm_sample(front, to_lanes(gc_s), c_all, n_all, m_all, lw)
    ys = _outproj(xs2d, oa_s, from_lanes(obt), from_lanes(oct), lw, tm_s)
    conv_new = uc_s.reshape(bs, ts, D_C)[:, ts - (CONV_W - 1):, :]
    res_s = (ys.reshape(bs, ts, d), ka_s.reshape(bs, ts, N_HEADS_A, HEAD_DIM),
             va_s.reshape(bs, ts, N_HEADS_A, HEAD_DIM), s_new_s, c_new_s, n_new_s, m_new_s, conv_new)
    return res_p, res_s


def kernel(x_prompt, x_sample, cache_k, cache_v, page_table, state_hgrn, state_mlstm_c, state_mlstm_n,
           state_mlstm_m, state_mlstm_conv, norm_g, w_in, q_norm_g, k_norm_g, hgrn_lb, hgrn_norm_g,
           mlstm_conv_w, mlstm_conv_b, mlstm_wq, mlstm_wk, mlstm_wv, mlstm_w_ig, mlstm_b_ig, mlstm_w_fg,
           mlstm_b_fg, mlstm_skip, mlstm_norm_g, w_out):
    depth = w_in.shape[0]
    bp, tp, _ = x_prompt.shape
    bd, td, _ = x_sample.shape
    n_pages = page_table.shape[1]
    page = cache_k.shape[2]
    chunk = min(tp, 256)
    shared = {
        'e256': _head_block_ones(),
        'hgrn_lb': hgrn_lb.astype(F32),
        'hgrn_consts': _hgrn_consts(chunk),
        'mlstm_tri': _mlstm_consts(chunk),
        'slopes': jnp.asarray(2.0 ** (-8.0 * (np.arange(N_HEADS_A) + 1) / N_HEADS_A), dtype=F32),
        'sample_consts': _moba_sample_consts(td, n_pages * page),
    }
    pt_flat = page_table.reshape(-1).astype(jnp.int32)
    cache_kt = jnp.transpose(cache_k, (0, 1, 3, 4, 2))
    cache_vt = jnp.transpose(cache_v, (0, 1, 3, 4, 2))
    s_h = jnp.transpose(state_hgrn, (0, 2, 3, 4, 1))
    s_c = jnp.transpose(state_mlstm_c, (0, 2, 3, 4, 1))
    s_n = jnp.transpose(state_mlstm_n, (0, 2, 3, 1))
    s_m = jnp.transpose(state_mlstm_m, (0, 2, 1))[:, :, None, :]
    s_cv = jnp.transpose(state_mlstm_conv, (0, 2, 3, 1))

    yp, ys = x_prompt, x_sample
    kv_prompt_buf = None
    outs_p = [[] for _ in range(7)]
    outs_s = [[] for _ in range(7)]
    stacked = _stacked_weights(norm_g, w_in, q_norm_g, k_norm_g, hgrn_norm_g, mlstm_conv_w, mlstm_conv_b,
                               mlstm_wq, mlstm_wk, mlstm_wv, mlstm_w_ig, mlstm_b_ig, mlstm_w_fg, mlstm_b_fg,
                               mlstm_skip, mlstm_norm_g, w_out)
    for l in range(depth):
        lw = _LayerWeights(stacked, l)
        res_p, res_s = _layer(yp, ys, l, lw, shared, chunk, min(PROMPT_ROW_TILE, tp),
                              math.gcd(SAMPLE_ROW_TILE, bd * td), depth,
                              kv_prompt_buf, cache_kt, cache_vt, pt_flat, s_h, s_c, s_n, s_m, s_cv)
        yp, ys = res_p[0], res_s[0]
        kv_prompt_buf = (res_p[1], res_p[2])
        for acc, a in zip(outs_p[2:], res_p[3:]):
            acc.append(a)
        for acc, a in zip(outs_s, res_s[1:]):
            acc.append(a)

    st = lambda lst, ax: jnp.stack(lst, axis=ax)

    def kv_prompt(buf):
        a = buf.reshape(bp, depth, N_HEADS_A, HEAD_DIM, tp)
        return jnp.transpose(a, (0, 1, 4, 2, 3))

    batch_first = lambda a: jnp.moveaxis(a, -1, 1)
    return (yp, ys, kv_prompt(kv_prompt_buf[0]), kv_prompt(kv_prompt_buf[1]), st(outs_s[0], 1), st(outs_s[1], 1),
            st(outs_p[2], 0), batch_first(st(outs_s[2], 0)), st(outs_p[3], 0), batch_first(st(outs_s[3], 0)),
            st(outs_p[4], 0), batch_first(st(outs_s[4], 0)), st(outs_p[5], 0),
            batch_first(st(outs_s[5], 0)[:, :, 0, :]), st(outs_p[6], 0), st(outs_s[6], 0))
```
